```python
import jax, jax.numpy as jnp
from jax import lax
import numpy as np

D_MODEL = 1024
BATCH = 8
SEQ = 4096
DEPTH = 2
DEC_BATCH = 32
DEC_SEQ = 1
PAST_LEN = 16384
PAGE_SIZE = 128

HEAD_DIM = 64
MIX_W = D_MODEL
NSA_HEADS = MIX_W // (2 * HEAD_DIM)
NSA_KV = 2
NSA_GROUP = NSA_HEADS // NSA_KV
CMP_BLOCK = 64
CMP_HIDDEN = 256
SEL_TOPN = 16
WINDOW = 512
NSA_QBLK = 32
SB_HEADS = MIX_W // (2 * HEAD_DIM)
SB_QBLK = 128
MOBA_HEADS = MIX_W // HEAD_DIM
MOBA_KV = 4
MOBA_GROUP = MOBA_HEADS // MOBA_KV
MOBA_BLOCK = 256
MOBA_TOPK = 3
MOBA_QBLK = 8
D_FF = 4 * D_MODEL
AB_PROJ = NSA_HEADS * HEAD_DIM + 6 * NSA_KV * HEAD_DIM + 3 * NSA_HEADS + 3 * SB_HEADS * HEAD_DIM
C_PROJ = MOBA_HEADS * HEAD_DIM + 2 * MOBA_KV * HEAD_DIM
EPS = 1e-6
NEG = -1e30
FORCE = 1e4
SCALE = HEAD_DIM ** -0.5

kernel_name = 'nsa_stickbreak_moba_hybrid_step'


def rmsnorm(x, g):
    xf = x.astype(jnp.float32)
    y = xf * lax.rsqrt(jnp.mean(xf * xf, axis=-1, keepdims=True) + EPS)
    return (y * g.astype(jnp.float32)).astype(x.dtype)


def alibi_slopes(n):
    return jnp.asarray([2.0 ** (-8.0 * (h + 1) / n) for h in range(n)], dtype=jnp.float32)


def masked_softmax(s, mask):
    p = jax.nn.softmax(jnp.where(mask, s, NEG), axis=-1)
    return jnp.where(mask, p, 0.0)


def sweep(step, n_q, blk):
    out = lax.map(step, jnp.arange(n_q // blk))
    out = jnp.moveaxis(out, 0, 1)
    return out.reshape((out.shape[0], n_q) + out.shape[3:])


def sq_relu_mlp(h, w_up, w_down):
    a = jax.nn.relu(h @ w_up)
    return (a * a) @ w_down


def compress_kv(kv, pos, w1, w2):
    B, S = kv.shape[:2]
    nb = S // CMP_BLOCK
    blk = kv.reshape(B, nb, CMP_BLOCK, 2, NSA_KV, HEAD_DIM) + jnp.transpose(pos, (1, 0, 2))[None, None, :, :, None, :]
    blk = jnp.transpose(blk, (0, 1, 3, 4, 2, 5)).reshape(B, nb, 2, NSA_KV, CMP_BLOCK * HEAD_DIM)
    hid = jax.nn.silu(jnp.einsum('bnkgf,kfh->bnkgh', blk, w1))
    return jnp.einsum('bnkgh,khd->bnkgd', hid, w2)


def nsa_core(q, gate, qpos, cmp_kv, n_sel, sel_gather, win_kv, win_pos, slopes):
    B, T = q.shape[:2]
    f32 = jnp.float32
    qg = q.reshape(B, T, NSA_KV, NSA_GROUP, HEAD_DIM)
    sl = slopes.reshape(1, 1, NSA_KV, NSA_GROUP, 1)
    tq = qpos.astype(f32).reshape(1, T, 1, 1, 1)
    n_cmp = cmp_kv.shape[1]
    cend = jnp.arange(n_cmp) * CMP_BLOCK + (CMP_BLOCK - 1)
    s = jnp.einsum('btgrd,bngd->btgrn', qg, cmp_kv[:, :, 0], preferred_element_type=f32) * SCALE
    s = s - sl * (tq - cend.astype(f32))
    vis = (cend[None, :] <= qpos[:, None])[None, :, None, None, :]
    p_cmp = masked_softmax(s, vis)
    o_cmp = jnp.einsum('btgrn,bngd->btgrd', p_cmp, cmp_kv[:, :, 1])
    imp = jnp.pad(p_cmp.sum(axis=3), ((0, 0), (0, 0), (0, 0), (0, n_sel - n_cmp)))
    j = jnp.arange(n_sel)[None, :]
    own = (qpos // CMP_BLOCK)[:, None]
    forced = (j == 0) | (j == own) | (j == own - 1)
    valid = j <= own
    score = jnp.where(forced[None, :, None, :], FORCE, imp)
    score = jnp.where(valid[None, :, None, :], score, -1.0)
    top_s, idx = lax.top_k(score, min(SEL_TOPN, n_sel))
    kv = sel_gather(idx)
    kpos = idx[..., None] * CMP_BLOCK + jnp.arange(CMP_BLOCK)
    s = jnp.einsum('btgrd,btgnkd->btgrnk', qg, kv[..., 0, :], preferred_element_type=f32) * SCALE
    dist = (qpos[None, :, None, None, None] - kpos).astype(f32)
    s = s - sl[..., None] * dist[:, :, :, None]
    mask = ((top_s >= 0)[..., None] & (kpos <= qpos[None, :, None, None, None]))[:, :, :, None]
    nk = s.shape[-2] * s.shape[-1]
    p = masked_softmax(s.reshape(B, T, NSA_KV, NSA_GROUP, nk), jnp.broadcast_to(mask, s.shape).reshape(B, T, NSA_KV, NSA_GROUP, nk))
    o_slc = jnp.einsum('btgrm,btgmd->btgrd', p, kv[..., 1, :].reshape(B, T, NSA_KV, nk, HEAD_DIM))
    s = jnp.einsum('btgrd,blgd->btgrl', qg, win_kv[:, :, 0], preferred_element_type=f32) * SCALE
    s = s - sl * (tq - win_pos.astype(f32))
    wm = (win_pos[None, :] <= qpos[:, None]) & (win_pos[None, :] > qpos[:, None] - WINDOW) & (win_pos[None, :] >= 0)
    p = masked_softmax(s, wm[None, :, None, None, :])
    o_win = jnp.einsum('btgrl,blgd->btgrd', p, win_kv[:, :, 1])
    g = gate.reshape(B, T, NSA_KV, NSA_GROUP, 3)
    o = g[..., 0:1] * o_cmp + g[..., 1:2] * o_slc + g[..., 2:3] * o_win
    return o.reshape(B, T, NSA_HEADS, HEAD_DIM)


def sb_core(q, k, v, qpos, kpos):
    z = jnp.einsum('bthd,bshd->bhts', q, k, preferred_element_type=jnp.float32) * SCALE
    vis = (kpos[None, :] < qpos[:, None])[None, None]
    log_stop = jnp.where(vis, jax.nn.log_sigmoid(-z), 0.0)
    later = lax.cumsum(log_stop, axis=3, reverse=True) - log_stop
    w = jnp.where(vis, jnp.exp(jax.nn.log_sigmoid(z) + later), 0.0)
    return jnp.einsum('bhts,bshd->bthd', w, v)


def moba_core(q, qpos, kvb, means, slopes):
    B, T = q.shape[:2]
    f32 = jnp.float32
    nb = kvb.shape[1]
    qg = q.reshape(B, T, MOBA_KV, MOBA_GROUP, HEAD_DIM)
    sl = slopes.reshape(1, 1, MOBA_KV, MOBA_GROUP, 1)
    tq = qpos.astype(f32)
    own = qpos // MOBA_BLOCK
    gsc = jnp.einsum('btgrd,bngd->btgrn', qg.astype(f32), means)
    past = (jnp.arange(nb)[None, :] < own[:, None])[None, :, None, None, :]
    kk = min(MOBA_TOPK, nb)
    _, idx = lax.top_k(jnp.where(past, gsc, NEG), kk)
    ok = idx < own[None, :, None, None, None]
    bi = jnp.arange(B)[:, None, None, None, None]
    gi = jnp.arange(MOBA_KV)[None, None, :, None, None]
    sel = kvb[bi, idx, :, :, gi, :]
    kpos = idx[..., None] * MOBA_BLOCK + jnp.arange(MOBA_BLOCK)
    s_sel = jnp.einsum('btgrd,btgrnkd->btgrnk', qg, sel[..., 0, :], preferred_element_type=f32) * SCALE
    s_sel = s_sel - sl[..., None] * (tq[None, :, None, None, None, None] - kpos.astype(f32))
    nsel = kk * MOBA_BLOCK
    m_sel = jnp.broadcast_to(ok[..., None], s_sel.shape).reshape(B, T, MOBA_KV, MOBA_GROUP, nsel)
    own_kv = kvb[jnp.arange(B)[:, None], own[None, :]]
    opos = own[:, None] * MOBA_BLOCK + jnp.arange(MOBA_BLOCK)
    s_own = jnp.einsum('btgrd,btkgd->btgrk', qg, own_kv[:, :, :, 0], preferred_element_type=f32) * SCALE
    s_own = s_own - sl * (tq[:, None] - opos.astype(f32))[None, :, None, None, :]
    m_own = jnp.broadcast_to((opos <= qpos[:, None])[None, :, None, None, :], s_own.shape)
    s = jnp.concatenate([s_sel.reshape(B, T, MOBA_KV, MOBA_GROUP, nsel), s_own], axis=-1)
    p = masked_softmax(s, jnp.concatenate([m_sel, m_own], axis=-1))
    o = jnp.einsum('btgrm,btgrmd->btgrd', p[..., :nsel], sel[..., 1, :].reshape(B, T, MOBA_KV, MOBA_GROUP, nsel, HEAD_DIM))
    o = o + jnp.einsum('btgrk,btkgd->btgrd', p[..., nsel:], own_kv[:, :, :, 1])
    return o.reshape(B, T, MOBA_HEADS, HEAD_DIM)


def ab_split(h, w_in):
    B, T, _ = h.shape
    sizes = [NSA_HEADS * HEAD_DIM] + [2 * NSA_KV * HEAD_DIM] * 3 + [3 * NSA_HEADS, SB_HEADS * HEAD_DIM, 2 * SB_HEADS * HEAD_DIM]
    parts = jnp.split(h @ w_in, np.cumsum(sizes)[:-1].tolist(), axis=-1)
    q_a = parts[0].reshape(B, T, NSA_HEADS, HEAD_DIM)
    kv_cmp = parts[1].reshape(B, T, 2, NSA_KV, HEAD_DIM)
    kv_slc = parts[2].reshape(B, T, 2, NSA_KV, HEAD_DIM)
    kv_win = parts[3].reshape(B, T, 2, NSA_KV, HEAD_DIM)
    gate = jax.nn.sigmoid(parts[4].astype(jnp.float32)).reshape(B, T, NSA_HEADS, 3)
    q_b = parts[5].reshape(B, T, SB_HEADS, HEAD_DIM)
    kv_b = parts[6].reshape(B, T, 2, SB_HEADS, HEAD_DIM)
    return q_a, kv_cmp, kv_slc, kv_win, gate, q_b, kv_b


def ab_prompt(h, w_in, w_out, c_pos, c_w1, c_w2):
    q_a, kv_cmp, kv_slc, kv_win, gate, q_b, kv_b = ab_split(h, w_in)
    B, S = h.shape[:2]
    slopes = alibi_slopes(NSA_HEADS)
    cmp = compress_kv(kv_cmp, c_pos, c_w1, c_w2)
    nb = S // CMP_BLOCK
    blocks = kv_slc.reshape(B, nb, CMP_BLOCK, 2, NSA_KV, HEAD_DIM)
    bi = jnp.arange(B)[:, None, None, None]
    gi = jnp.arange(NSA_KV)[None, None, :, None]

    def gather(idx):
        return blocks[bi, idx, :, :, gi, :]

    win_pad = jnp.pad(kv_win, ((0, 0), (WINDOW, 0), (0, 0), (0, 0), (0, 0)))

    def step_a(i):
        t0 = i * NSA_QBLK
        qpos = t0 + jnp.arange(NSA_QBLK)
        wpos = t0 - WINDOW + jnp.arange(WINDOW + NSA_QBLK)
        cut = lambda a, n: lax.dynamic_slice_in_dim(a, t0, n, axis=1)
        return nsa_core(cut(q_a, NSA_QBLK), cut(gate, NSA_QBLK), qpos, cmp, nb, gather,
                        cut(win_pad, WINDOW + NSA_QBLK), wpos, slopes)

    o_a = sweep(step_a, S, NSA_QBLK)
    k_b, v_b = kv_b[:, :, 0], kv_b[:, :, 1]
    kpos = jnp.arange(S)

    def step_b(i):
        t0 = i * SB_QBLK
        return sb_core(lax.dynamic_slice_in_dim(q_b, t0, SB_QBLK, axis=1), k_b, v_b, t0 + jnp.arange(SB_QBLK), kpos)

    o_b = sweep(step_b, S, SB_QBLK)
    o = jnp.concatenate([o_a, o_b], axis=2).reshape(B, S, MIX_W).astype(h.dtype) @ w_out
    return o, kv_slc, cmp, kv_win[:, S - min(WINDOW, S):], kv_b


def ab_sample(h, w_in, w_out, c_pos, c_w1, c_w2, c_slc, c_cmp, c_win, c_sb, page_table):
    q_a, kv_cmp, kv_slc, kv_win, gate, q_b, kv_b = ab_split(h, w_in)
    B, T = h.shape[:2]
    n_pages = page_table.shape[1]
    n_past = n_pages * PAGE_SIZE
    bpp = PAGE_SIZE // CMP_BLOCK
    qpos = n_past + jnp.arange(T)
    slopes = alibi_slopes(NSA_HEADS)
    cmp = c_cmp[page_table].reshape(B, n_pages * bpp, 2, NSA_KV, HEAD_DIM)
    n_full = T // CMP_BLOCK
    if n_full > 0:
        cmp = jnp.concatenate([cmp, compress_kv(kv_cmp[:, :n_full * CMP_BLOCK], c_pos, c_w1, c_w2)], axis=1)
    n_past_blk = n_pages * bpp
    n_new_blk = -(-T // CMP_BLOCK)
    new_blocks = jnp.pad(kv_slc, ((0, 0), (0, n_new_blk * CMP_BLOCK - T), (0, 0), (0, 0), (0, 0)))
    new_blocks = new_blocks.reshape(B, n_new_blk, CMP_BLOCK, 2, NSA_KV, HEAD_DIM)
    pool = c_slc.reshape(-1, CMP_BLOCK, 2, NSA_KV, HEAD_DIM)
    bi = jnp.arange(B)[:, None, None, None]
    gi = jnp.arange(NSA_KV)[None, None, :, None]

    def gather(idx):
        is_new = idx >= n_past_blk
        pi = jnp.minimum(idx, n_past_blk - 1)
        phys = page_table[bi, pi // bpp] * bpp + pi % bpp
        old = pool[phys, :, :, gi, :]
        new = new_blocks[bi, jnp.clip(idx - n_past_blk, 0, n_new_blk - 1), :, :, gi, :]
        return jnp.where(is_new[..., None, None, None], new, old)

    win_kv = jnp.concatenate([c_win, kv_win], axis=1)
    L = win_kv.shape[1]
    win_pos = n_past - c_win.shape[1] + jnp.arange(L)
    o_a = nsa_core(q_a, gate, qpos, cmp, n_past_blk + n_new_blk, gather, win_kv, win_pos, slopes)
    kv_all = jnp.concatenate([c_sb[page_table].reshape(B, n_past, 2, SB_HEADS, HEAD_DIM), kv_b], axis=1)
    o_b = sb_core(q_b, kv_all[:, :, 0], kv_all[:, :, 1], qpos, jnp.arange(n_past + T))
    o = jnp.concatenate([o_a, o_b], axis=2).reshape(B, T, MIX_W).astype(h.dtype) @ w_out
    return o, kv_slc, kv_cmp, win_kv[:, L - min(WINDOW, L):], kv_b


def c_split(h, w_in):
    B, T, _ = h.shape
    q, kv = jnp.split(h @ w_in, [MOBA_HEADS * HEAD_DIM], axis=-1)
    return q.reshape(B, T, MOBA_HEADS, HEAD_DIM), kv.reshape(B, T, 2, MOBA_KV, HEAD_DIM)


def moba_blocks(kv):
    B, S = kv.shape[:2]
    nb = -(-S // MOBA_BLOCK)
    kv = jnp.pad(kv, ((0, 0), (0, nb * MOBA_BLOCK - S), (0, 0), (0, 0), (0, 0)))
    kvb = kv.reshape(B, nb, MOBA_BLOCK, 2, MOBA_KV, HEAD_DIM)
    means = jnp.mean(kvb[:, :, :, 0].astype(jnp.float32), axis=2)
    return kvb, means


def c_prompt(h, w_in, w_out):
    q, kv = c_split(h, w_in)
    B, S = h.shape[:2]
    kvb, means = moba_blocks(kv)
    slopes = alibi_slopes(MOBA_HEADS)

    def step(i):
        t0 = i * MOBA_QBLK
        return moba_core(lax.dynamic_slice_in_dim(q, t0, MOBA_QBLK, axis=1), t0 + jnp.arange(MOBA_QBLK), kvb, means, slopes)

    o = sweep(step, S, MOBA_QBLK)
    return o.reshape(B, S, MIX_W).astype(h.dtype) @ w_out, kv


def c_sample(h, w_in, w_out, c_moba, page_table):
    q, kv = c_split(h, w_in)
    B, T = h.shape[:2]
    n_past = page_table.shape[1] * PAGE_SIZE
    past = c_moba[page_table].reshape(B, n_past, 2, MOBA_KV, HEAD_DIM)
    kvb, means = moba_blocks(jnp.concatenate([past, kv], axis=1))
    o = moba_core(q, n_past + jnp.arange(T), kvb, means, alibi_slopes(MOBA_HEADS))
    return o.reshape(B, T, MIX_W).astype(h.dtype) @ w_out, kv


def setup_inputs(seed: int = 0) -> dict:
    key = jax.random.key(seed)
    ks = jax.random.split(key, 20)
    f32 = jnp.float32
    n_pages = PAST_LEN // PAGE_SIZE
    n_phys = (5 * DEC_BATCH * n_pages + 3) // 4
    win_buf = min(WINDOW, PAST_LEN)
    bpp = PAGE_SIZE // CMP_BLOCK
    n_even = (DEPTH + 1) // 2
    n_odd = DEPTH // 2

    def nrm(k, shape, scale):
        return jax.random.normal(k, shape, f32) * scale

    perm = jax.random.permutation(ks[7], n_phys).astype(jnp.int32)
    page_table = perm[: DEC_BATCH * n_pages].reshape(DEC_BATCH, n_pages)
    return {
        'x_prompt': nrm(ks[0], (BATCH, SEQ, D_MODEL), 1.0),
        'x_sample': nrm(ks[1], (DEC_BATCH, DEC_SEQ, D_MODEL), 1.0),
        'cache_nsa_slc': nrm(ks[2], (n_even, n_phys, PAGE_SIZE, 2, NSA_KV, HEAD_DIM), 1.0),
        'cache_nsa_cmp': nrm(ks[3], (n_even, n_phys, bpp, 2, NSA_KV, HEAD_DIM), 0.5),
        'cache_nsa_win': nrm(ks[4], (n_even, DEC_BATCH, win_buf, 2, NSA_KV, HEAD_DIM), 1.0),
        'cache_sb': nrm(ks[5], (n_even, n_phys, PAGE_SIZE, 2, SB_HEADS, HEAD_DIM), 1.0),
        'cache_moba': nrm(ks[6], (n_odd, n_phys, PAGE_SIZE, 2, MOBA_KV, HEAD_DIM), 1.0),
        'page_table': page_table,
        'norm_mix': 1.0 + nrm(ks[8], (DEPTH, D_MODEL), 0.01),
        'w_in_ab': nrm(ks[9], (n_even, D_MODEL, AB_PROJ), D_MODEL ** -0.5),
        'w_out_ab': nrm(ks[10], (n_even, MIX_W, D_MODEL), MIX_W ** -0.5),
        'cmp_pos': nrm(ks[11], (n_even, 2, CMP_BLOCK, HEAD_DIM), 0.5),
        'cmp_w1': nrm(ks[12], (n_even, 2, CMP_BLOCK * HEAD_DIM, CMP_HIDDEN), (CMP_BLOCK * HEAD_DIM) ** -0.5),
        'cmp_w2': nrm(ks[13], (n_even, 2, CMP_HIDDEN, HEAD_DIM), CMP_HIDDEN ** -0.5),
        'w_in_c': nrm(ks[14], (n_odd, D_MODEL, C_PROJ), D_MODEL ** -0.5),
        'w_out_c': nrm(ks[15], (n_odd, MIX_W, D_MODEL), MIX_W ** -0.5),
        'norm_ffn': 1.0 + nrm(ks[16], (DEPTH, D_MODEL), 0.01),
        'w_up': nrm(ks[17], (DEPTH, D_MODEL, D_FF), D_MODEL ** -0.5),
        'w_down': nrm(ks[18], (DEPTH, D_FF, D_MODEL), D_FF ** -0.5),
        'norm_final': 1.0 + nrm(ks[19], (D_MODEL,), 0.01),
    }


def reference(x_prompt, x_sample, cache_nsa_slc, cache_nsa_cmp, cache_nsa_win, cache_sb, cache_moba, page_table,
              norm_mix, w_in_ab, w_out_ab, cmp_pos, cmp_w1, cmp_w2, w_in_c, w_out_c,
              norm_ffn, w_up, w_down, norm_final):
    yp, ys = x_prompt, x_sample
    p_slc, p_cmp, p_win, p_sb, p_moba = [], [], [], [], []
    s_slc, s_cmp, s_win, s_sb, s_moba = [], [], [], [], []
    for layer in range(DEPTH):
        i = layer // 2
        hp = rmsnorm(yp, norm_mix[layer])
        hs = rmsnorm(ys, norm_mix[layer])
        if layer % 2 == 0:
            op, a, b, c, d = ab_prompt(hp, w_in_ab[i], w_out_ab[i], cmp_pos[i], cmp_w1[i], cmp_w2[i])
            p_slc.append(a); p_cmp.append(b); p_win.append(c); p_sb.append(d)
            os_, a, b, c, d = ab_sample(hs, w_in_ab[i], w_out_ab[i], cmp_pos[i], cmp_w1[i], cmp_w2[i],
                                        cache_nsa_slc[i], cache_nsa_cmp[i], cache_nsa_win[i], cache_sb[i], page_table)
            s_slc.append(a); s_cmp.append(b); s_win.append(c); s_sb.append(d)
        else:
            op, a = c_prompt(hp, w_in_c[i], w_out_c[i])
            p_moba.append(a)
            os_, a = c_sample(hs, w_in_c[i], w_out_c[i], cache_moba[i], page_table)
            s_moba.append(a)
        yp = yp + op
        ys = ys + os_
        yp = yp + sq_relu_mlp(rmsnorm(yp, norm_ffn[layer]), w_up[layer], w_down[layer])
        ys = ys + sq_relu_mlp(rmsnorm(ys, norm_ffn[layer]), w_up[layer], w_down[layer])
    yp = rmsnorm(yp, norm_final)
    ys = rmsnorm(ys, norm_final)
    return (yp, ys, jnp.stack(p_slc), jnp.stack(p_cmp), jnp.stack(p_win), jnp.stack(p_sb), jnp.stack(p_moba),
            jnp.stack(s_slc), jnp.stack(s_cmp), jnp.stack(s_win), jnp.stack(s_sb), jnp.stack(s_moba))
```

```python
import functools

import jax
import jax.numpy as jnp
from jax import lax
from jax.experimental import pallas as pl
from jax.experimental.pallas import tpu as pltpu

HEAD_DIM = 64
NSA_HEADS = 8
NSA_KV = 2
NSA_GROUP = NSA_HEADS // NSA_KV
CMP_BLOCK = 64
CMP_HIDDEN = 256
SEL_TOPN = 16
WINDOW = 512
SB_HEADS = 8
MOBA_HEADS = 16
MOBA_KV = 4
MOBA_GROUP = MOBA_HEADS // MOBA_KV
MOBA_BLOCK = 256
MOBA_TOPK = 3
PAGE_SIZE = 128
EPS = 1e-6
NEG = -1e30
FORCE = 1e4
SCALE = HEAD_DIM ** -0.5

F32 = jnp.float32
BF16 = jnp.bfloat16
HI = lax.Precision.HIGHEST

VMEM_LIMIT = 48 * 1024 * 1024

NSA_TQ = 128
NSA_TK = 128
SB_TQ = 256
SB_TK = 256


def _cparams(sem):
    return pltpu.CompilerParams(dimension_semantics=sem, vmem_limit_bytes=VMEM_LIMIT)


def _alibi_slopes(n):
    return [2.0 ** (-8.0 * (h + 1) / n) for h in range(n)]


def _nt(a, b, precision=None):
    return lax.dot_general(a, b, (((1,), (1,)), ((), ())), precision=precision,
                           preferred_element_type=F32)


def _rms(x, g):
    return x * lax.rsqrt(jnp.mean(x * x, axis=-1, keepdims=True) + EPS) * g


def _proj_kernel(x_ref, g_ref, w_ref, *o_refs, widths):
    h = _rms(x_ref[...], g_ref[...]).astype(BF16)
    off = 0
    for o_ref, wd in zip(o_refs, widths):
        o_ref[...] = jnp.dot(h, w_ref[:, off:off + wd], preferred_element_type=F32)
        off += wd


def rms_proj(x, g, w_bf16, widths, tm):
    m, d = x.shape
    n = w_bf16.shape[1]
    assert sum(widths) == n and m % tm == 0
    return pl.pallas_call(
        functools.partial(_proj_kernel, widths=widths),
        grid=(m // tm,),
        in_specs=[pl.BlockSpec((tm, d), lambda i: (i, 0)),
                  pl.BlockSpec((1, d), lambda i: (0, 0)),
                  pl.BlockSpec((d, n), lambda i: (0, 0))],
        out_specs=[pl.BlockSpec((tm, wd), lambda i: (i, 0)) for wd in widths],
        out_shape=[jax.ShapeDtypeStruct((m, wd), F32) for wd in widths],
        compiler_params=_cparams(("parallel",)),
        name="rms_proj",
    )(x, g.reshape(1, d), w_bf16)


def _mix_mlp_kernel(y_ref, o_ref, wo_ref, g_ref, wu_ref, wd_ref, gf_ref, out_ref, *, ff_chunk, final):
    y = y_ref[...] + jnp.dot(o_ref[...], wo_ref[...], preferred_element_type=F32)
    h = _rms(y, g_ref[...]).astype(BF16)
    acc = y
    for c in range(0, wu_ref.shape[1], ff_chunk):
        a = jnp.maximum(jnp.dot(h, wu_ref[:, c:c + ff_chunk], preferred_element_type=F32), 0.0)
        acc = acc + jnp.dot((a * a).astype(BF16), wd_ref[c:c + ff_chunk, :], preferred_element_type=F32)
    if final:
        acc = _rms(acc, gf_ref[...])
    out_ref[...] = acc


def mix_mlp(y, o_bf16, w_out, g_ffn, w_up, w_down, g_final, final, tm):
    m, d = y.shape
    f = w_up.shape[1]
    const = lambda i: (0, 0)
    return pl.pallas_call(
        functools.partial(_mix_mlp_kernel, ff_chunk=min(f, 1024), final=final),
        grid=(m // tm,),
        in_specs=[pl.BlockSpec((tm, d), lambda i: (i, 0)),
                  pl.BlockSpec((tm, d), lambda i: (i, 0)),
                  pl.BlockSpec((d, d), const),
                  pl.BlockSpec((1, d), const),
                  pl.BlockSpec((d, f), const),
                  pl.BlockSpec((f, d), const),
                  pl.BlockSpec((1, d), const)],
        out_specs=pl.BlockSpec((tm, d), lambda i: (i, 0)),
        out_shape=jax.ShapeDtypeStruct((m, d), F32),
        compiler_params=_cparams(("parallel",)),
        name="mix_mlp",
    )(y, o_bf16, w_out, g_ffn.reshape(1, d), w_up, w_down, g_final.reshape(1, d))


def _compress_kernel(x_ref, pos_ref, w1_ref, w2_ref, o_ref):
    x = (x_ref[...] + pos_ref[...]).astype(BF16)
    hid = jnp.dot(x, w1_ref[...], preferred_element_type=F32)
    hid = hid * (1.0 / (1.0 + jnp.exp(-hid)))
    o_ref[...] = jnp.dot(hid.astype(BF16), w2_ref[...], preferred_element_type=F32)


def compress_blocks(x, pos, w1_bf16, w2_bf16, tr):
    _, r, fdim = x.shape
    hdim = w1_bf16.shape[2]
    return pl.pallas_call(
        _compress_kernel,
        grid=(2, r // tr),
        in_specs=[pl.BlockSpec((None, tr, fdim), lambda k, i: (k, i, 0)),
                  pl.BlockSpec((None, 1, fdim), lambda k, i: (k, 0, 0)),
                  pl.BlockSpec((None, fdim, hdim), lambda k, i: (k, 0, 0)),
                  pl.BlockSpec((None, hdim, HEAD_DIM), lambda k, i: (k, 0, 0))],
        out_specs=pl.BlockSpec((None, tr, HEAD_DIM), lambda k, i: (k, i, 0)),
        out_shape=jax.ShapeDtypeStruct((2, r, HEAD_DIM), F32),
        compiler_params=_cparams(("parallel", "parallel")),
        name="nsa_compress",
    )(x, pos, w1_bf16, w2_bf16)


def _online_update(carry, s, msk, vt):
    m, l, acc = carry
    s = jnp.where(msk, s, NEG)
    m_new = jnp.maximum(m, jnp.max(s, axis=0, keepdims=True))
    alpha = jnp.exp(m - m_new)
    p = jnp.where(msk, jnp.exp(s - m_new), 0.0)
    l = alpha * l + jnp.sum(p, axis=0, keepdims=True)
    acc = alpha * acc + jnp.dot(vt, p.astype(BF16), preferred_element_type=F32)
    return m_new, l, acc


def _nsa_prompt_kernel(qt_ref, gate_ref, kc_ref, vct_ref, ks_ref, vst_ref, kw_ref, vwt_ref, slope_ref,
                       o_ref, sel_ref, *, nb):
    i = pl.program_id(2)
    tq = NSA_TQ
    lanes = NSA_GROUP * tq
    qt = qt_ref[...]
    qb = qt.astype(BF16)
    slope = slope_ref[...]
    lane = lax.broadcasted_iota(jnp.int32, (1, lanes), 1)
    t_row = i * tq + lane % tq
    tf = t_row.astype(F32)

    cend = lax.broadcasted_iota(jnp.int32, (nb, 1), 0) * CMP_BLOCK + (CMP_BLOCK - 1)
    sc = jnp.dot(kc_ref[...], qt, precision=HI, preferred_element_type=F32) * SCALE
    sc = sc - slope * (tf - cend.astype(F32))
    vis = cend <= t_row
    sc = jnp.where(vis, sc, NEG)
    e = jnp.where(vis, jnp.exp(sc - jnp.max(sc, axis=0, keepdims=True)), 0.0)
    den = jnp.sum(e, axis=0, keepdims=True)
    p = e * jnp.where(den > 0.0, 1.0 / den, 0.0)
    o_cmp = jnp.dot(vct_ref[...], p, precision=HI, preferred_element_type=F32)

    imp = p[:, 0:tq]
    for r in range(1, NSA_GROUP):
        imp = imp + p[:, r * tq:(r + 1) * tq]
    j = lax.broadcasted_iota(jnp.int32, (nb, tq), 0)
    own = (i * tq + lax.broadcasted_iota(jnp.int32, (nb, tq), 1)) // CMP_BLOCK
    forced = (j == 0) | (j == own) | (j == own - 1)
    score = jnp.where(forced, FORCE, imp)
    score = jnp.where(j <= own, score, -1.0)
    rank = jnp.zeros((nb, tq), F32)
    for m in range(nb):
        sm = score[m:m + 1, :]
        beats = (sm > score) | ((sm == score) & (j > m))
        rank = rank + jnp.where(beats, 1.0, 0.0)
    sel = jnp.where((rank < float(min(SEL_TOPN, nb))) & (score >= 0.0), 1.0, 0.0)
    sel_ref[...] = jnp.concatenate([sel] * NSA_GROUP, axis=1)

    sub = lax.broadcasted_iota(jnp.int32, (NSA_TK, 1), 0)
    init = (jnp.full((1, lanes), NEG, F32), jnp.zeros((1, lanes), F32), jnp.zeros((HEAD_DIM, lanes), F32))

    def scores(k_ref, c):
        kpos = c * NSA_TK + sub
        s = jnp.dot(k_ref[c], qb, preferred_element_type=F32) * SCALE
        return s - slope * (tf - kpos.astype(F32)), kpos

    def slc_body(c, carry):
        s, kpos = scores(ks_ref, c)
        row_a = sel_ref[pl.ds(2 * c, 1), :]
        row_b = sel_ref[pl.ds(2 * c + 1, 1), :]
        picked = jnp.where(sub < CMP_BLOCK, row_a, row_b) > 0.5
        return _online_update(carry, s, picked & (kpos <= t_row), vst_ref[c])

    _, l_s, acc_s = lax.fori_loop(0, i + 1, slc_body, init)

    def win_body(c, carry):
        s, kpos = scores(kw_ref, c)
        return _online_update(carry, s, (kpos <= t_row) & (kpos > t_row - WINDOW), vwt_ref[c])

    _, l_w, acc_w = lax.fori_loop(jnp.maximum(i - WINDOW // NSA_TK, 0), i + 1, win_body, init)

    gate = 1.0 / (1.0 + jnp.exp(-gate_ref[...]))
    o_ref[...] = gate[0:1] * o_cmp + gate[1:2] * (acc_s / l_s) + gate[2:3] * (acc_w / l_w)


def nsa_prompt(qt, gate_t, kc, vct, ks, vst, kw, vwt, slopes):
    b, g, nt, d, lanes = qt.shape
    nb = kc.shape[2]
    nc = ks.shape[2]
    blk5 = lambda *shape: pl.BlockSpec((None, None, None) + shape, lambda bi, gi, i: (bi, gi, i, 0, 0))
    res4 = lambda *shape: pl.BlockSpec((None, None) + shape, lambda bi, gi, i: (bi, gi, 0, 0))
    res5 = lambda *shape: pl.BlockSpec((None, None) + shape, lambda bi, gi, i: (bi, gi, 0, 0, 0))
    return pl.pallas_call(
        functools.partial(_nsa_prompt_kernel, nb=nb),
        grid=(b, g, nt),
        in_specs=[blk5(d, lanes), blk5(8, lanes),
                  res4(nb, d), res4(d, nb),
                  res5(nc, NSA_TK, d), res5(nc, d, NSA_TK),
                  res5(nc, NSA_TK, d), res5(nc, d, NSA_TK),
                  pl.BlockSpec((None, 1, lanes), lambda bi, gi, i: (gi, 0, 0))],
        out_specs=blk5(d, lanes),
        out_shape=jax.ShapeDtypeStruct((b, g, nt, d, lanes), F32),
        scratch_shapes=[pltpu.VMEM((nb, lanes), F32)],
        compiler_params=_cparams(("parallel", "parallel", "arbitrary")),
        name="nsa_prompt",
    )(qt, gate_t, kc, vct, ks, vst, kw, vwt, slopes)


def _softplus(z):
    return jnp.maximum(z, 0.0) + jnp.log(1.0 + jnp.exp(-jnp.abs(z)))


def _sb_prompt_kernel(q_ref, k_ref, v_ref, o_ref):
    i = pl.program_id(2)
    tq, tk = SB_TQ, SB_TK
    q = q_ref[...]
    qpos = i * tq + lax.broadcasted_iota(jnp.int32, (tq, 1), 0)
    upper = jnp.where(lax.broadcasted_iota(jnp.int32, (tk, tk), 0) > lax.broadcasted_iota(jnp.int32, (tk, tk), 1),
                      1.0, 0.0).astype(BF16)

    def body(step, carry):
        acc, run = carry
        ks = pl.multiple_of((i - step) * tk, tk)
        k = k_ref[pl.ds(ks, tk), :]
        v = v_ref[pl.ds(ks, tk), :]
        z = _nt(q, k) * SCALE
        vis = (ks + lax.broadcasted_iota(jnp.int32, (1, tk), 1)) < qpos
        sp = _softplus(z)
        ls = jnp.where(vis, -sp, 0.0)
        hi = ls.astype(BF16)
        lo = (ls - hi.astype(F32)).astype(BF16)
        later = (jnp.dot(hi, upper, preferred_element_type=F32)
                 + jnp.dot(lo, upper, preferred_element_type=F32) + run)
        w = jnp.where(vis, jnp.exp(z - sp + later), 0.0)
        acc = acc + jnp.dot(w.astype(BF16), v, preferred_element_type=F32)
        return acc, run + jnp.sum(ls, axis=1, keepdims=True)

    acc, _ = lax.fori_loop(0, i + 1, body, (jnp.zeros((tq, HEAD_DIM), F32), jnp.zeros((tq, 1), F32)))
    o_ref[...] = acc


def sb_prompt(q, k, v):
    b, h, s, d = q.shape
    return pl.pallas_call(
        _sb_prompt_kernel,
        grid=(b, h, s // SB_TQ),
        in_specs=[pl.BlockSpec((None, None, SB_TQ, d), lambda bi, hi, i: (bi, hi, i, 0)),
                  pl.BlockSpec((None, None, s, d), lambda bi, hi, i: (bi, hi, 0, 0)),
                  pl.BlockSpec((None, None, s, d), lambda bi, hi, i: (bi, hi, 0, 0))],
        out_specs=pl.BlockSpec((None, None, SB_TQ, d), lambda bi, hi, i: (bi, hi, i, 0)),
        out_shape=jax.ShapeDtypeStruct((b, h, s, d), F32),
        compiler_params=_cparams(("parallel", "parallel", "arbitrary")),
        name="sb_prompt",
    )(q, k, v)


AB_WIDTHS = (NSA_HEADS * HEAD_DIM, 2 * NSA_KV * HEAD_DIM, 2 * NSA_KV * HEAD_DIM, 2 * NSA_KV * HEAD_DIM,
             128, SB_HEADS * HEAD_DIM, 2 * SB_HEADS * HEAD_DIM)
N_GATE = 3 * NSA_HEADS


def _pad_w_in_ab(w):
    g0 = sum(AB_WIDTHS[:4])
    return jnp.concatenate([w[:, :g0], jnp.pad(w[:, g0:g0 + N_GATE], ((0, 0), (0, 128 - N_GATE))),
                            w[:, g0 + N_GATE:]], axis=1).astype(BF16)


def _group_slopes(n_heads, n_kv, tq):
    sl = jnp.asarray(_alibi_slopes(n_heads), F32).reshape(n_kv, n_heads // n_kv, 1)
    return jnp.broadcast_to(sl, (n_kv, n_heads // n_kv, tq)).reshape(n_kv, 1, (n_heads // n_kv) * tq)


def _chunked_kv(kv, tk):
    b, s, _, g, d = kv.shape
    k = jnp.transpose(kv[:, :, 0].reshape(b, s // tk, tk, g, d), (0, 3, 1, 2, 4)).astype(BF16)
    vt = jnp.transpose(kv[:, :, 1].reshape(b, s // tk, tk, g, d), (0, 3, 1, 4, 2)).astype(BF16)
    return k, vt


def ab_prompt_mix(y, g_mix, w_in_pad, c_pos, c_w1, c_w2, bsz, seq, tm):
    m = bsz * seq
    q_a, kv_cmp, kv_slc, kv_win, gate, q_b, kv_b = rms_proj(y, g_mix, w_in_pad, AB_WIDTHS, tm)
    nb = seq // CMP_BLOCK
    nt = seq // NSA_TQ
    kv_cmp5 = kv_cmp.reshape(bsz, seq, 2, NSA_KV, HEAD_DIM)
    kv_slc5 = kv_slc.reshape(bsz, seq, 2, NSA_KV, HEAD_DIM)
    kv_win5 = kv_win.reshape(bsz, seq, 2, NSA_KV, HEAD_DIM)
    kv_b5 = kv_b.reshape(bsz, seq, 2, SB_HEADS, HEAD_DIM)

    x = jnp.transpose(kv_cmp5.reshape(bsz, nb, CMP_BLOCK, 2, NSA_KV, HEAD_DIM), (3, 0, 1, 4, 2, 5))
    x = x.reshape(2, bsz * nb * NSA_KV, CMP_BLOCK * HEAD_DIM)
    rows = bsz * nb * NSA_KV
    cmp = compress_blocks(x, c_pos.reshape(2, 1, CMP_BLOCK * HEAD_DIM), c_w1.astype(BF16), c_w2.astype(BF16),
                          min(rows, 256))
    cmp = cmp.reshape(2, bsz, nb, NSA_KV, HEAD_DIM)
    p_cmp = jnp.transpose(cmp, (1, 2, 0, 3, 4))
    kc = jnp.transpose(cmp[0], (0, 2, 1, 3))
    vct = jnp.transpose(cmp[1], (0, 2, 3, 1))

    qt = jnp.transpose(q_a.reshape(bsz, nt, NSA_TQ, NSA_KV, NSA_GROUP, HEAD_DIM), (0, 3, 1, 5, 4, 2))
    qt = qt.reshape(bsz, NSA_KV, nt, HEAD_DIM, NSA_GROUP * NSA_TQ)
    gt = jnp.transpose(gate[:, :N_GATE].reshape(bsz, nt, NSA_TQ, NSA_KV, NSA_GROUP, 3), (0, 3, 1, 5, 4, 2))
    gt = jnp.pad(gt.reshape(bsz, NSA_KV, nt, 3, NSA_GROUP * NSA_TQ), ((0, 0), (0, 0), (0, 0), (0, 5), (0, 0)))
    ks, vst = _chunked_kv(kv_slc5, NSA_TK)
    kw, vwt = _chunked_kv(kv_win5, NSA_TK)
    o_at = nsa_prompt(qt, gt, kc, vct, ks, vst, kw, vwt, _group_slopes(NSA_HEADS, NSA_KV, NSA_TQ))
    o_a = jnp.transpose(o_at.reshape(bsz, NSA_KV, nt, HEAD_DIM, NSA_GROUP, NSA_TQ), (0, 2, 5, 1, 4, 3))
    o_a = o_a.reshape(bsz, seq, NSA_HEADS * HEAD_DIM)

    qb_h = jnp.transpose(q_b.reshape(bsz, seq, SB_HEADS, HEAD_DIM), (0, 2, 1, 3)).astype(BF16)
    kb_h = jnp.transpose(kv_b5[:, :, 0], (0, 2, 1, 3)).astype(BF16)
    vb_h = jnp.transpose(kv_b5[:, :, 1], (0, 2, 1, 3)).astype(BF16)
    o_b = jnp.transpose(sb_prompt(qb_h, kb_h, vb_h), (0, 2, 1, 3)).reshape(bsz, seq, SB_HEADS * HEAD_DIM)

    o = jnp.concatenate([o_a, o_b], axis=-1).reshape(m, -1).astype(BF16)
    return o, kv_slc5, p_cmp, kv_win5[:, seq - min(WINDOW, seq):], kv_b5


def _block_mean_kernel(k_ref, o_ref):
    o_ref[...] = jnp.sum(k_ref[...], axis=0, keepdims=True) * (1.0 / MOBA_BLOCK)


def moba_block_means(kv, bsz, seq):
    nblk = seq // MOBA_BLOCK
    width = MOBA_KV * HEAD_DIM
    return pl.pallas_call(
        _block_mean_kernel,
        grid=(bsz, nblk),
        in_specs=[pl.BlockSpec((MOBA_BLOCK, width), lambda b, n: (b * nblk + n, 0))],
        out_specs=pl.BlockSpec((None, None, 1, width), lambda b, n: (b, n, 0, 0)),
        out_shape=jax.ShapeDtypeStruct((bsz, nblk, 1, width), F32),
        compiler_params=_cparams(("parallel", "parallel")),
        name="moba_block_means",
    )(kv)


def _moba_prompt_kernel(qt_ref, means_ref, k_ref, vt_ref, slope_ref, o_ref, sel_ref, *, nblk):
    i = pl.program_id(2)
    tq = MOBA_BLOCK
    lanes = MOBA_GROUP * tq
    qt = qt_ref[...]
    qb = qt.astype(BF16)
    slope = slope_ref[...]
    lane = lax.broadcasted_iota(jnp.int32, (1, lanes), 1)
    t_row = i * tq + lane % tq
    tf = t_row.astype(F32)

    gsc = jnp.dot(means_ref[...], qt, precision=HI, preferred_element_type=F32)
    n_idx = lax.broadcasted_iota(jnp.int32, (nblk, lanes), 0)
    past = n_idx < i
    gsc = jnp.where(past, gsc, NEG)
    rank = jnp.zeros((nblk, lanes), F32)
    for m in range(nblk):
        gm = gsc[m:m + 1, :]
        beats = (gm > gsc) | ((gm == gsc) & (n_idx > m))
        rank = rank + jnp.where(beats, 1.0, 0.0)
    sel_ref[...] = jnp.where((rank < float(min(MOBA_TOPK, nblk))) & past, 1.0, 0.0)

    sub = lax.broadcasted_iota(jnp.int32, (tq, 1), 0)

    def scores(n):
        kpos = n * tq + sub
        s = jnp.dot(k_ref[n], qb, preferred_element_type=F32) * SCALE
        return s - slope * (tf - kpos.astype(F32)), kpos

    init = (jnp.full((1, lanes), NEG, F32), jnp.zeros((1, lanes), F32), jnp.zeros((HEAD_DIM, lanes), F32))
    s_own, kpos_own = scores(i)
    carry = _online_update(init, s_own, kpos_own <= t_row, vt_ref[i])

    def body(n, carry):
        s, _ = scores(n)
        picked = sel_ref[pl.ds(n, 1), :] > 0.5
        return _online_update(carry, s, jnp.broadcast_to(picked, s.shape), vt_ref[n])

    _, l, acc = lax.fori_loop(0, i, body, carry)
    o_ref[...] = acc / l


def moba_prompt(qt, means, k, vt, slopes):
    b, g, nqb, d, lanes = qt.shape
    nblk = k.shape[2]
    return pl.pallas_call(
        functools.partial(_moba_prompt_kernel, nblk=nblk),
        grid=(b, g, nqb),
        in_specs=[pl.BlockSpec((None, None, None, d, lanes), lambda bi, gi, i: (bi, gi, i, 0, 0)),
                  pl.BlockSpec((None, None, nblk, d), lambda bi, gi, i: (bi, gi, 0, 0)),
                  pl.BlockSpec((None, None, nblk, MOBA_BLOCK, d), lambda bi, gi, i: (bi, gi, 0, 0, 0)),
                  pl.BlockSpec((None, None, nblk, d, MOBA_BLOCK), lambda bi, gi, i: (bi, gi, 0, 0, 0)),
                  pl.BlockSpec((None, 1, lanes), lambda bi, gi, i: (gi, 0, 0))],
        out_specs=pl.BlockSpec((None, None, None, d, lanes), lambda bi, gi, i: (bi, gi, i, 0, 0)),
        out_shape=jax.ShapeDtypeStruct((b, g, nqb, d, lanes), F32),
        scratch_shapes=[pltpu.VMEM((nblk, lanes), F32)],
        compiler_params=_cparams(("parallel", "parallel", "arbitrary")),
        name="moba_prompt",
    )(qt, means, k, vt, slopes)


C_WIDTHS = (MOBA_HEADS * HEAD_DIM, 2 * MOBA_KV * HEAD_DIM)


def c_prompt_mix(y, g_mix, w_in_bf16, bsz, seq, tm):
    m = bsz * seq
    q, kv = rms_proj(y, g_mix, w_in_bf16, C_WIDTHS, tm)
    nblk = seq // MOBA_BLOCK
    kv5 = kv.reshape(bsz, seq, 2, MOBA_KV, HEAD_DIM)
    means = moba_block_means(kv, bsz, seq).reshape(bsz, nblk, MOBA_KV, HEAD_DIM)
    means = jnp.transpose(means, (0, 2, 1, 3))
    qt = jnp.transpose(q.reshape(bsz, nblk, MOBA_BLOCK, MOBA_KV, MOBA_GROUP, HEAD_DIM), (0, 3, 1, 5, 4, 2))
    qt = qt.reshape(bsz, MOBA_KV, nblk, HEAD_DIM, MOBA_GROUP * MOBA_BLOCK)
    k, vt = _chunked_kv(kv5, MOBA_BLOCK)
    o_t = moba_prompt(qt, means, k, vt, _group_slopes(MOBA_HEADS, MOBA_KV, MOBA_BLOCK))
    o = jnp.transpose(o_t.reshape(bsz, MOBA_KV, nblk, HEAD_DIM, MOBA_GROUP, MOBA_BLOCK), (0, 2, 5, 1, 4, 3))
    return o.reshape(m, MOBA_HEADS * HEAD_DIM).astype(BF16), kv5


PAGES_PER_STEP = 8
N_SEL_OLD = SEL_TOPN - 1
BIG_IDX = 1 << 30


def _head_slopes(n_heads):
    return jnp.broadcast_to(jnp.asarray(_alibi_slopes(n_heads), F32).reshape(n_heads, 1), (n_heads, 128))


def _fold_heads(wide, n_groups, heads_per_group):
    n_heads = n_groups * heads_per_group
    hrow = lax.broadcasted_iota(jnp.int32, (n_heads, HEAD_DIM), 0) // heads_per_group
    out = jnp.zeros((n_heads, HEAD_DIM), F32)
    for g in range(n_groups):
        out = out + jnp.where(hrow == g, wide[:, g * HEAD_DIM:(g + 1) * HEAD_DIM], 0.0)
    return out


def _nsa_select_kernel(pt_ref, qm_ref, slope_ref, ccmp_ref, ocmp_ref, idx_ref, buf_ref, sem, *, n_pages):
    b = pl.program_id(0)
    row_copy = lambda p, page: pltpu.make_async_copy(ccmp_ref.at[pl.ds(page, 1), :], buf_ref.at[pl.ds(p, 1), :], sem)

    def issue(p, c):
        row_copy(p, pt_ref[b, p]).start()
        return c

    def drain(p, c):
        row_copy(p, 0).wait()
        return c

    lax.fori_loop(0, n_pages, issue, 0)
    lax.fori_loop(0, n_pages, drain, 0)

    buf = buf_ref[...]
    slope = slope_ref[:, 0:1]
    qpos = float(n_pages * PAGE_SIZE)
    page = lax.broadcasted_iota(jnp.int32, (1, n_pages), 1)
    st = _nt(qm_ref[...], buf, HI) * SCALE
    halves = []
    for blk in range(2):
        cend = (page * PAGE_SIZE + blk * CMP_BLOCK + (CMP_BLOCK - 1)).astype(F32)
        halves.append(st[blk * NSA_HEADS:(blk + 1) * NSA_HEADS] - slope * (qpos - cend))
    mx = jnp.maximum(jnp.max(halves[0], axis=1, keepdims=True), jnp.max(halves[1], axis=1, keepdims=True))
    e = [jnp.exp(hv - mx) for hv in halves]
    inv = 1.0 / (jnp.sum(e[0], axis=1, keepdims=True) + jnp.sum(e[1], axis=1, keepdims=True))
    p = [ev * inv for ev in e]
    v_lo = NSA_KV * HEAD_DIM
    ocmp_ref[...] = (jnp.dot(p[0], buf[:, v_lo:2 * v_lo], precision=HI, preferred_element_type=F32)
                     + jnp.dot(p[1], buf[:, 3 * v_lo:4 * v_lo], precision=HI, preferred_element_type=F32))

    hrow = lax.broadcasted_iota(jnp.int32, (NSA_HEADS, n_pages), 0) // NSA_GROUP
    jidx = (2 * lax.broadcasted_iota(jnp.int32, (2, n_pages), 1)
            + lax.broadcasted_iota(jnp.int32, (2, n_pages), 0)).astype(F32)
    lane = lax.broadcasted_iota(jnp.int32, (1, 128), 1)
    n_blk = 2 * n_pages
    rows = []
    for g in range(NSA_KV):
        imp = jnp.concatenate([jnp.sum(jnp.where(hrow == g, pv, 0.0), axis=0, keepdims=True) for pv in p], axis=0)
        cand = jnp.where((jidx == 0.0) | (jidx == float(n_blk - 1)), -1.0, imp)
        vec = jnp.where(lane == N_SEL_OLD - 1, float(n_blk - 1), 0.0)
        for r in range(N_SEL_OLD - 2):
            best = jnp.max(jnp.max(cand, axis=1, keepdims=True), axis=0, keepdims=True)
            hit = jnp.where(cand == best, jidx, float(BIG_IDX))
            pick = jnp.min(jnp.min(hit, axis=1, keepdims=True), axis=0, keepdims=True)
            vec = jnp.where(lane == r, pick, vec)
            cand = jnp.where(jidx == pick, -2.0, cand)
        rows.append(vec)
    idx_ref[...] = jnp.concatenate(rows + [jnp.zeros((8 - NSA_KV, 128), F32)], axis=0).astype(jnp.int32)


def nsa_sample_select(page_table, qm_cmp, c_cmp):
    bsz, n_pages = page_table.shape
    width = c_cmp.shape[1]
    assert 2 * n_pages - 2 >= N_SEL_OLD - 2
    return pl.pallas_call(
        functools.partial(_nsa_select_kernel, n_pages=n_pages),
        grid_spec=pltpu.PrefetchScalarGridSpec(
            num_scalar_prefetch=1,
            grid=(bsz,),
            in_specs=[pl.BlockSpec((None, 2 * NSA_HEADS, width), lambda b, pt: (b, 0, 0)),
                      pl.BlockSpec((NSA_HEADS, 128), lambda b, pt: (0, 0)),
                      pl.BlockSpec(memory_space=pl.ANY)],
            out_specs=[pl.BlockSpec((None, NSA_HEADS, 128), lambda b, pt: (b, 0, 0)),
                       pl.BlockSpec((None, 8, 128), lambda b, pt: (b, 0, 0))],
            scratch_shapes=[pltpu.VMEM((n_pages, width), F32), pltpu.SemaphoreType.DMA(())]),
        out_shape=[jax.ShapeDtypeStruct((bsz, NSA_HEADS, 128), F32),
                   jax.ShapeDtypeStruct((bsz, 8, 128), jnp.int32)],
        compiler_params=_cparams(("arbitrary",)),
        name="nsa_sample_select",
    )(page_table, qm_cmp, _head_slopes(NSA_HEADS), c_cmp)


def _nsa_attend_kernel(pt_ref, idx_ref, qm_ref, slope_ref, gate_ref, ocmp_ref, new_slc_ref, new_win_ref, cwin_ref,
                       cslc_ref, o_ref, swin_ref, buf_ref, sem, *, n_pages):
    b = pl.program_id(0)

    def block_copy(g, r):
        j = idx_ref[b, g * N_SEL_OLD + r]
        page = pt_ref[b, j // 2]
        return pltpu.make_async_copy(cslc_ref.at[page, pl.ds((j % 2) * CMP_BLOCK, CMP_BLOCK), :],
                                     buf_ref.at[g, pl.ds(r * CMP_BLOCK, CMP_BLOCK), :], sem)

    for g in range(NSA_KV):
        for r in range(N_SEL_OLD):
            block_copy(g, r).start()
    for g in range(NSA_KV):
        for r in range(N_SEL_OLD):
            block_copy(g, r).wait()

    qm = qm_ref[...]
    slope = slope_ref[:, 0:1]
    qpos = n_pages * PAGE_SIZE
    v_lo = NSA_KV * HEAD_DIM
    n_keys = N_SEL_OLD * CMP_BLOCK
    hrow = lax.broadcasted_iota(jnp.int32, (NSA_HEADS, 1), 0) // NSA_GROUP
    lane = lax.broadcasted_iota(jnp.int32, (1, n_keys), 1)

    def finish(s, s_new, values, v_new):
        mx = jnp.maximum(jnp.max(s, axis=1, keepdims=True), s_new)
        e = jnp.exp(s - mx)
        e_new = jnp.exp(s_new - mx)
        acc = e_new * v_new
        for g, val in enumerate(values):
            eg = e if len(values) == 1 else jnp.where(hrow == g, e, 0.0)
            acc = acc + jnp.dot(eg, val, precision=HI, preferred_element_type=F32)
        return acc / (jnp.sum(e, axis=1, keepdims=True) + e_new)

    s_groups, values = [], []
    for g in range(NSA_KV):
        bufg = buf_ref[g]
        kpos = lane % CMP_BLOCK
        for r in range(N_SEL_OLD):
            kpos = kpos + jnp.where(lane // CMP_BLOCK == r, idx_ref[b, g * N_SEL_OLD + r] * CMP_BLOCK, 0)
        s_groups.append(_nt(qm, bufg, HI) * SCALE - slope * (qpos - kpos).astype(F32))
        values.append(bufg[:, v_lo:2 * v_lo])
    s_slc = jnp.where(hrow == 0, s_groups[0], s_groups[1])
    new_slc = new_slc_ref[...]
    s_new = jnp.sum(qm * new_slc, axis=1, keepdims=True) * SCALE
    o_slc = finish(s_slc, s_new, values, new_slc[:, v_lo:2 * v_lo])

    cw = cwin_ref[...]
    wlen = cw.shape[0]
    wl = lax.broadcasted_iota(jnp.int32, (1, wlen), 1)
    s_w = _nt(qm, cw, HI) * SCALE - slope * (wlen - wl).astype(F32)
    s_w = jnp.where(wl >= 1, s_w, NEG)
    new_win = new_win_ref[...]
    s_new_w = jnp.sum(qm * new_win, axis=1, keepdims=True) * SCALE
    o_win = finish(s_w, s_new_w, [cw[:, v_lo:2 * v_lo]], new_win[:, v_lo:2 * v_lo])
    swin_ref[0:wlen - 1, :] = cw[1:wlen, :]
    swin_ref[wlen - 1:wlen, :] = new_win

    gate = 1.0 / (1.0 + jnp.exp(-gate_ref[...]))
    wide = gate[:, 0:1] * ocmp_ref[...] + gate[:, 1:2] * o_slc + gate[:, 2:3] * o_win
    o_ref[...] = _fold_heads(wide, NSA_KV, NSA_GROUP)


def nsa_sample_attend(page_table, idx, qm_kv, gate, o_cmp, new_slc, new_win, c_win, c_slc):
    bsz, n_pages = page_table.shape
    wlen, width = c_win.shape[1:]
    per_b = lambda *shape: pl.BlockSpec((None,) + shape, lambda b, pt, ix: (b, 0, 0))
    return pl.pallas_call(
        functools.partial(_nsa_attend_kernel, n_pages=n_pages),
        grid_spec=pltpu.PrefetchScalarGridSpec(
            num_scalar_prefetch=2,
            grid=(bsz,),
            in_specs=[per_b(NSA_HEADS, width),
                      pl.BlockSpec((NSA_HEADS, 128), lambda b, pt, ix: (0, 0)),
                      per_b(NSA_HEADS, 128), per_b(NSA_HEADS, 128), per_b(1, width), per_b(1, width),
                      per_b(wlen, width),
                      pl.BlockSpec(memory_space=pl.ANY)],
            out_specs=[per_b(NSA_HEADS, HEAD_DIM), per_b(wlen, width)],
            scratch_shapes=[pltpu.VMEM((NSA_KV, N_SEL_OLD * CMP_BLOCK, width), F32), pltpu.SemaphoreType.DMA(())]),
        out_shape=[jax.ShapeDtypeStruct((bsz, NSA_HEADS, HEAD_DIM), F32),
                   jax.ShapeDtypeStruct((bsz, wlen, width), F32)],
        compiler_params=_cparams(("arbitrary",)),
        name="nsa_sample_attend",
    )(page_table, idx, qm_kv, _head_slopes(NSA_HEADS), gate, o_cmp, new_slc, new_win, c_win, c_slc)


def _sb_sample_kernel(pt_ref, qm_ref, *refs):
    page_refs = refs[:PAGES_PER_STEP]
    o_ref, acc_ref, run_ref = refs[PAGES_PER_STEP:]
    s = pl.program_id(1)

    @pl.when(s == 0)
    def _():
        acc_ref[...] = jnp.zeros_like(acc_ref)
        run_ref[...] = jnp.zeros_like(run_ref)

    qm = qm_ref[...]
    width = SB_HEADS * HEAD_DIM
    tk = PAGE_SIZE
    upper = jnp.where(lax.broadcasted_iota(jnp.int32, (tk, tk), 0) > lax.broadcasted_iota(jnp.int32, (tk, tk), 1),
                      1.0, 0.0).astype(BF16)
    acc = acc_ref[...]
    run = run_ref[:, 0:1]
    for page_ref in page_refs:
        z = _nt(qm, page_ref[:, 0:width].astype(BF16)) * SCALE
        sp = _softplus(z)
        ls = -sp
        hi = ls.astype(BF16)
        lo = (ls - hi.astype(F32)).astype(BF16)
        later = (jnp.dot(hi, upper, preferred_element_type=F32)
                 + jnp.dot(lo, upper, preferred_element_type=F32) + run)
        w = jnp.exp(z - sp + later)
        acc = acc + jnp.dot(w.astype(BF16), page_ref[:, width:2 * width].astype(BF16), preferred_element_type=F32)
        run = run + jnp.sum(ls, axis=1, keepdims=True)
    acc_ref[...] = acc
    run_ref[...] = jnp.broadcast_to(run, run_ref.shape)

    @pl.when(s == pl.num_programs(1) - 1)
    def _():
        own = (lax.broadcasted_iota(jnp.int32, (SB_HEADS, width), 1) // HEAD_DIM
               == lax.broadcasted_iota(jnp.int32, (SB_HEADS, width), 0))
        o_ref[...] = jnp.sum(jnp.where(own, acc, 0.0), axis=0, keepdims=True)


def sb_sample(page_table, qm_bf16, c_sb):
    bsz, n_pages = page_table.shape
    width2 = c_sb.shape[2]
    assert n_pages % PAGES_PER_STEP == 0

    def page_spec(u):
        return pl.BlockSpec((None, PAGE_SIZE, width2),
                            lambda b, s, pt: (pt[b, n_pages - 1 - (s * PAGES_PER_STEP + u)], 0, 0))

    return pl.pallas_call(
        _sb_sample_kernel,
        grid_spec=pltpu.PrefetchScalarGridSpec(
            num_scalar_prefetch=1,
            grid=(bsz, n_pages // PAGES_PER_STEP),
            in_specs=[pl.BlockSpec((None, SB_HEADS, width2 // 2), lambda b, s, pt: (b, 0, 0))]
                     + [page_spec(u) for u in range(PAGES_PER_STEP)],
            out_specs=pl.BlockSpec((None, 1, width2 // 2), lambda b, s, pt: (b, 0, 0)),
            scratch_shapes=[pltpu.VMEM((SB_HEADS, width2 // 2), F32), pltpu.VMEM((SB_HEADS, 128), F32)]),
        out_shape=jax.ShapeDtypeStruct((bsz, 1, width2 // 2), F32),
        compiler_params=_cparams(("parallel", "arbitrary")),
        name="sb_sample",
    )(page_table, qm_bf16, *([c_sb] * PAGES_PER_STEP))


def _moba_sample_kernel(pt_ref, qm_ref, slope_ref, new_ref, *refs, n_pages):
    page_refs = refs[:PAGES_PER_STEP]
    o_ref, obuf_ref, m_ref, l_ref, means_ref = refs[PAGES_PER_STEP:]
    s = pl.program_id(1)
    width = MOBA_KV * HEAD_DIM
    qm = qm_ref[...]
    qb = qm.astype(BF16)
    slope = slope_ref[:, 0:1]
    qpos = n_pages * PAGE_SIZE
    lane = lax.broadcasted_iota(jnp.int32, (1, PAGE_SIZE), 1)
    plane = lax.broadcasted_iota(jnp.int32, (1, n_pages), 1)

    @pl.when(s == 0)
    def _():
        m_ref[...] = jnp.zeros_like(m_ref)
        l_ref[...] = jnp.zeros_like(l_ref)

    m_all = m_ref[...]
    l_all = l_ref[...]
    ksums = []
    for u, page_ref in enumerate(page_refs):
        pg = s * PAGES_PER_STEP + u
        kf = page_ref[:, 0:width]
        sc = _nt(qb, kf.astype(BF16)) * SCALE - slope * (qpos - (pg * PAGE_SIZE + lane)).astype(F32)
        mx = jnp.max(sc, axis=1, keepdims=True)
        e = jnp.exp(sc - mx)
        obuf_ref[pg] = jnp.dot(e.astype(BF16), page_ref[:, width:2 * width].astype(BF16),
                               preferred_element_type=F32)
        m_all = jnp.where(plane == pg, mx, m_all)
        l_all = jnp.where(plane == pg, jnp.sum(e, axis=1, keepdims=True), l_all)
        ksums.append(jnp.sum(kf, axis=0, keepdims=True))
    m_ref[...] = m_all
    l_ref[...] = l_all
    pages_per_block = MOBA_BLOCK // PAGE_SIZE
    for u in range(0, PAGES_PER_STEP, pages_per_block):
        mean = (ksums[u] + ksums[u + 1]) * (1.0 / MOBA_BLOCK)
        for v in range(pages_per_block):
            means_ref[pl.ds(s * PAGES_PER_STEP + u + v, 1), :] = mean

    @pl.when(s == pl.num_programs(1) - 1)
    def _():
        gsc = _nt(qm, means_ref[...], HI)
        cand = gsc
        blk_of_page = (plane // pages_per_block).astype(F32)
        sel = jnp.zeros(gsc.shape, jnp.bool_)
        for _ in range(MOBA_TOPK):
            best = jnp.max(cand, axis=1, keepdims=True)
            first = jnp.min(jnp.where(cand == best, blk_of_page, float(BIG_IDX)), axis=1, keepdims=True)
            hit = blk_of_page == first
            sel = sel | hit
            cand = jnp.where(hit, NEG, cand)
        new = new_ref[...]
        s_new = jnp.sum(qm * new[:, 0:width], axis=1, keepdims=True) * SCALE
        top = jnp.maximum(jnp.max(jnp.where(sel, m_all, NEG), axis=1, keepdims=True), s_new)
        coef = jnp.where(sel, jnp.exp(m_all - top), 0.0)
        e_new = jnp.exp(s_new - top)
        den = jnp.sum(coef * l_all, axis=1, keepdims=True) + e_new
        wide = e_new * new[:, width:2 * width]
        for p in range(n_pages):
            wide = wide + coef[:, p:p + 1] * obuf_ref[p]
        o_ref[...] = _fold_heads(wide, MOBA_KV, MOBA_GROUP) / den


def moba_sample(page_table, qm, new_kv, c_moba):
    bsz, n_pages = page_table.shape
    width2 = c_moba.shape[2]
    assert n_pages % PAGES_PER_STEP == 0 and n_pages * PAGE_SIZE >= MOBA_TOPK * MOBA_BLOCK

    def page_spec(u):
        return pl.BlockSpec((None, PAGE_SIZE, width2), lambda b, s, pt: (pt[b, s * PAGES_PER_STEP + u], 0, 0))

    return pl.pallas_call(
        functools.partial(_moba_sample_kernel, n_pages=n_pages),
        grid_spec=pltpu.PrefetchScalarGridSpec(
            num_scalar_prefetch=1,
            grid=(bsz, n_pages // PAGES_PER_STEP),
            in_specs=[pl.BlockSpec((None, MOBA_HEADS, width2 // 2), lambda b, s, pt: (b, 0, 0)),
                      pl.BlockSpec((MOBA_HEADS, 128), lambda b, s, pt: (0, 0)),
                      pl.BlockSpec((None, 1, width2), lambda b, s, pt: (b, 0, 0))]
                     + [page_spec(u) for u in range(PAGES_PER_STEP)],
            out_specs=pl.BlockSpec((None, MOBA_HEADS, HEAD_DIM), lambda b, s, pt: (b, 0, 0)),
            scratch_shapes=[pltpu.VMEM((n_pages, MOBA_HEADS, width2 // 2), F32),
                            pltpu.VMEM((MOBA_HEADS, n_pages), F32),
                            pltpu.VMEM((MOBA_HEADS, n_pages), F32),
                            pltpu.VMEM((n_pages, width2 // 2), F32)]),
        out_shape=jax.ShapeDtypeStruct((bsz, MOBA_HEADS, HEAD_DIM), F32),
        compiler_params=_cparams(("parallel", "arbitrary")),
        name="moba_sample",
    )(page_table, qm, _head_slopes(MOBA_HEADS), new_kv, *([c_moba] * PAGES_PER_STEP))


def _block_diag_query(q, n_groups):
    bsz, n_heads, d = q.shape
    onehot = (jnp.arange(n_heads)[:, None] // (n_heads // n_groups) == jnp.arange(n_groups)[None, :]).astype(q.dtype)
    return (q[:, :, None, :] * onehot[None, :, :, None]).reshape(bsz, n_heads, n_groups * d)


def ab_sample_mix(y, g_mix, w_in_pad, c_slc, c_cmp, c_win, c_sb, page_table):
    bsz = y.shape[0]
    n_phys = c_slc.shape[0]
    q_a, kv_cmp, kv_slc, kv_win, gate, q_b, kv_b = rms_proj(y, g_mix, w_in_pad, AB_WIDTHS, bsz)
    qm_kv = jnp.pad(_block_diag_query(q_a.reshape(bsz, NSA_HEADS, HEAD_DIM), NSA_KV),
                    ((0, 0), (0, 0), (0, NSA_KV * HEAD_DIM)))
    zeros = jnp.zeros_like(qm_kv)
    qm_cmp = jnp.concatenate([jnp.concatenate([qm_kv, zeros], -1), jnp.concatenate([zeros, qm_kv], -1)], axis=1)
    o_cmp, idx = nsa_sample_select(page_table, qm_cmp, c_cmp.reshape(n_phys, -1))
    gate3 = jnp.pad(gate[:, :N_GATE].reshape(bsz, NSA_HEADS, 3), ((0, 0), (0, 0), (0, 125)))
    width = 2 * NSA_KV * HEAD_DIM
    o_a, s_win = nsa_sample_attend(page_table, idx[:, :NSA_KV, :N_SEL_OLD].reshape(bsz, -1), qm_kv, gate3, o_cmp,
                                   kv_slc.reshape(bsz, 1, width), kv_win.reshape(bsz, 1, width),
                                   c_win.reshape(bsz, -1, width), c_slc.reshape(n_phys, PAGE_SIZE, width))
    qm_sb = _block_diag_query(q_b.reshape(bsz, SB_HEADS, HEAD_DIM), SB_HEADS).astype(BF16)
    o_b = sb_sample(page_table, qm_sb, c_sb.reshape(n_phys, PAGE_SIZE, -1))
    o = jnp.concatenate([o_a.reshape(bsz, -1), o_b.reshape(bsz, -1)], axis=-1).astype(BF16)
    shape5 = (bsz, 1, 2, NSA_KV, HEAD_DIM)
    return (o, kv_slc.reshape(shape5), kv_cmp.reshape(shape5), s_win.reshape(bsz, -1, 2, NSA_KV, HEAD_DIM),
            kv_b.reshape(bsz, 1, 2, SB_HEADS, HEAD_DIM))


def c_sample_mix(y, g_mix, w_in_bf16, c_moba, page_table):
    bsz = y.shape[0]
    n_phys = c_moba.shape[0]
    q, kv = rms_proj(y, g_mix, w_in_bf16, C_WIDTHS, bsz)
    qm = _block_diag_query(q.reshape(bsz, MOBA_HEADS, HEAD_DIM), MOBA_KV)
    o = moba_sample(page_table, qm, kv.reshape(bsz, 1, -1), c_moba.reshape(n_phys, PAGE_SIZE, -1))
    return o.reshape(bsz, -1).astype(BF16), kv.reshape(bsz, 1, 2, MOBA_KV, HEAD_DIM)


def kernel(x_prompt, x_sample, cache_nsa_slc, cache_nsa_cmp, cache_nsa_win, cache_sb, cache_moba, page_table,
           norm_mix, w_in_ab, w_out_ab, cmp_pos, cmp_w1, cmp_w2, w_in_c, w_out_c,
           norm_ffn, w_up, w_down, norm_final):
    bsz, seq, d = x_prompt.shape
    tm = 512
    w_up_b = w_up.astype(BF16)
    w_down_b = w_down.astype(BF16)
    dbsz = x_sample.shape[0]
    yp = x_prompt.reshape(bsz * seq, d)
    ys = x_sample.reshape(dbsz, d)
    w_ab = _pad_w_in_ab(w_in_ab[0])
    w_oab = w_out_ab[0].astype(BF16)
    w_c = w_in_c[0].astype(BF16)
    w_oc = w_out_c[0].astype(BF16)

    o, p_slc, p_cmp, p_win, p_sb = ab_prompt_mix(yp, norm_mix[0], w_ab, cmp_pos[0], cmp_w1[0], cmp_w2[0],
                                                 bsz, seq, tm)
    yp = mix_mlp(yp, o, w_oab, norm_ffn[0], w_up_b[0], w_down_b[0], norm_final, False, tm)
    o, p_moba = c_prompt_mix(yp, norm_mix[1], w_c, bsz, seq, tm)
    yp = mix_mlp(yp, o, w_oc, norm_ffn[1], w_up_b[1], w_down_b[1], norm_final, True, tm)

    o, s_slc, s_cmp, s_win, s_sb = ab_sample_mix(ys, norm_mix[0], w_ab, cache_nsa_slc[0], cache_nsa_cmp[0],
                                                 cache_nsa_win[0], cache_sb[0], page_table)
    ys = mix_mlp(ys, o, w_oab, norm_ffn[0], w_up_b[0], w_down_b[0], norm_final, False, dbsz)
    o, s_moba = c_sample_mix(ys, norm_mix[1], w_c, cache_moba[0], page_table)
    ys = mix_mlp(ys, o, w_oc, norm_ffn[1], w_up_b[1], w_down_b[1], norm_final, True, dbsz)

    return (yp.reshape(bsz, seq, d), ys.reshape(dbsz, 1, d), p_slc[None], p_cmp[None], p_win[None], p_sb[None],
            p_moba[None], s_slc[None], s_cmp[None], s_win[None], s_sb[None], s_moba[None])
```

```python
import functools

import jax
import jax.numpy as jnp
from jax import lax
from jax.experimental import pallas as pl
from jax.experimental.pallas import tpu as pltpu

HEAD_DIM = 64
NSA_HEADS = 8
NSA_KV = 2
NSA_GROUP = NSA_HEADS // NSA_KV
CMP_BLOCK = 64
SEL_TOPN = 16
WINDOW = 512
SB_HEADS = 8
MOBA_HEADS = 16
MOBA_KV = 4
MOBA_GROUP = MOBA_HEADS // MOBA_KV
MOBA_BLOCK = 256
MOBA_TOPK = 3
PAGE_SIZE = 128
EPS = 1e-6
NEG = -1e30
MASK_BIAS = -(2.0 ** 100)
FORCE = 1e4
SCALE = HEAD_DIM ** -0.5

F32 = jnp.float32
BF16 = jnp.bfloat16
HI = lax.Precision.HIGHEST

VMEM_LIMIT = 48 * 1024 * 1024

NSA_TQ = 128
NSA_TK = 256
SB_T = 256
AUG = 16
N_GATE = 3 * NSA_HEADS


def _cparams(sem):
    return pltpu.CompilerParams(dimension_semantics=sem, vmem_limit_bytes=VMEM_LIMIT)


def _alibi_slopes(n):
    return [2.0 ** (-8.0 * (h + 1) / n) for h in range(n)]


def _nt(a, b, precision=None):
    return lax.dot_general(a, b, (((1,), (1,)), ((), ())), precision=precision, preferred_element_type=F32)


def _tn(a, b):
    return lax.dot_general(a, b, (((0,), (0,)), ((), ())), preferred_element_type=F32)


def _dot(a, b, precision=None):
    return jnp.dot(a, b, precision=precision, preferred_element_type=F32)


def _rms(x, g):
    return x * lax.rsqrt(jnp.mean(x * x, axis=-1, keepdims=True) + EPS) * g


def _softplus(z):
    return jnp.maximum(z, 0.0) + jnp.log(1.0 + jnp.exp(-jnp.abs(z)))


def _iota(shape, axis):
    return lax.broadcasted_iota(jnp.int32, shape, axis)


def _proj_kernel(*refs, std, tr):
    x_ref, g_ref = refs[:2]
    refs = refs[2:]
    ws_ref = wt_ref = None
    if std:
        ws_ref, refs = refs[0], refs[1:]
    if tr:
        wt_ref, refs = refs[0], refs[1:]
    h = _rms(x_ref[...], g_ref[...]).astype(BF16)
    k = 0
    for off, wd, _ in std:
        refs[k][...] = _dot(h, ws_ref[:, off:off + wd]).astype(refs[k].dtype)
        k += 1
    for off, rows, _ in tr:
        refs[k][...] = _nt(wt_ref[off:off + rows, :], h).astype(refs[k].dtype)
        k += 1


def rms_proj(x, g, w_std, std, w_t, tr, bsz, seq, tm):
    m, d = x.shape
    nj = seq // tm
    assert m == bsz * seq and seq % tm == 0
    const = lambda i: (0, 0)
    operands = [x, g.reshape(1, d)]
    in_specs = [pl.BlockSpec((tm, d), lambda i: (i, 0)), pl.BlockSpec((1, d), const)]
    if std:
        operands.append(w_std)
        in_specs.append(pl.BlockSpec(w_std.shape, const))
    if tr:
        operands.append(w_t)
        in_specs.append(pl.BlockSpec(w_t.shape, const))
    out_specs = ([pl.BlockSpec((tm, wd), lambda i: (i, 0)) for _, wd, _ in std]
                 + [pl.BlockSpec((None, rows, tm), lambda i: (i // nj, 0, i % nj)) for _, rows, _ in tr])
    out_shape = ([jax.ShapeDtypeStruct((m, wd), dt) for _, wd, dt in std]
                 + [jax.ShapeDtypeStruct((bsz, rows, seq), dt) for _, rows, dt in tr])
    return pl.pallas_call(
        functools.partial(_proj_kernel, std=tuple(std), tr=tuple(tr)),
        grid=(m // tm,), in_specs=in_specs, out_specs=out_specs, out_shape=out_shape,
        compiler_params=_cparams(("parallel",)), name="rms_proj",
    )(*operands)


def _mix_mlp_kernel(y_ref, o_ref, wo_ref, g_ref, wu_ref, wd_ref, gf_ref, out_ref, *, ff_chunk, final):
    y = y_ref[...] + _tn(o_ref[...], wo_ref[...])
    h = _rms(y, g_ref[...]).astype(BF16)
    acc = y
    for c in range(0, wu_ref.shape[1], ff_chunk):
        a = jnp.maximum(_dot(h, wu_ref[:, c:c + ff_chunk]), 0.0)
        acc = acc + _dot((a * a).astype(BF16), wd_ref[c:c + ff_chunk, :])
    if final:
        acc = _rms(acc, gf_ref[...])
    out_ref[...] = acc


def mix_mlp(y, o_t, w_out, g_ffn, w_up, w_down, g_final, final, bsz, seq, tm):
    m, d = y.shape
    f = w_up.shape[1]
    nj = seq // tm
    const = lambda i: (0, 0)
    return pl.pallas_call(
        functools.partial(_mix_mlp_kernel, ff_chunk=min(f, 1024), final=final),
        grid=(m // tm,),
        in_specs=[pl.BlockSpec((tm, d), lambda i: (i, 0)),
                  pl.BlockSpec((None, o_t.shape[1], tm), lambda i: (i // nj, 0, i % nj)),
                  pl.BlockSpec((d, d), const), pl.BlockSpec((1, d), const), pl.BlockSpec((d, f), const),
                  pl.BlockSpec((f, d), const), pl.BlockSpec((1, d), const)],
        out_specs=pl.BlockSpec((tm, d), lambda i: (i, 0)),
        out_shape=jax.ShapeDtypeStruct((m, d), F32),
        compiler_params=_cparams(("parallel",)), name="mix_mlp",
    )(y, o_t, w_out, g_ffn.reshape(1, d), w_up, w_down, g_final.reshape(1, d))


def _compress_kernel(x_ref, pos_ref, w1_ref, w2_ref, o_ref):
    x = (x_ref[...] + pos_ref[...]).astype(BF16)
    hid = _dot(x, w1_ref[...])
    hid = hid * (1.0 / (1.0 + jnp.exp(-hid)))
    o_ref[...] = _dot(hid.astype(BF16), w2_ref[...])


def compress_blocks(x, pos, w1_bf16, w2_bf16, tr):
    _, r, fdim = x.shape
    hdim = w1_bf16.shape[2]
    return pl.pallas_call(
        _compress_kernel,
        grid=(2, r // tr),
        in_specs=[pl.BlockSpec((None, tr, fdim), lambda k, i: (k, i, 0)),
                  pl.BlockSpec((None, 1, fdim), lambda k, i: (k, 0, 0)),
                  pl.BlockSpec((None, fdim, hdim), lambda k, i: (k, 0, 0)),
                  pl.BlockSpec((None, hdim, HEAD_DIM), lambda k, i: (k, 0, 0))],
        out_specs=pl.BlockSpec((None, tr, HEAD_DIM), lambda k, i: (k, i, 0)),
        out_shape=jax.ShapeDtypeStruct((2, r, HEAD_DIM), F32),
        compiler_params=_cparams(("parallel", "parallel")), name="nsa_compress",
    )(x, pos, w1_bf16, w2_bf16)


def _biased_update(carry, s, shift, vt):
    m, l, acc = carry
    m_new = jnp.maximum(m, jnp.max(s, axis=0, keepdims=True) + shift)
    alpha = jnp.exp(m - m_new)
    p = jnp.exp(s - (m_new - shift))
    l = alpha * l + jnp.sum(p, axis=0, keepdims=True)
    acc = alpha * acc + _dot(vt, p.astype(BF16))
    return m_new, l, acc


def _slope_rows(slope, lanes):
    p1 = slope.astype(BF16).astype(F32)
    p2 = (slope - p1).astype(BF16).astype(F32)
    p3 = slope - p1 - p2
    return jnp.concatenate([p1, p2, p3, jnp.zeros((5, lanes), F32)], axis=0)


def _softmax_init(lanes):
    return (jnp.full((1, lanes), NEG, F32), jnp.zeros((1, lanes), F32), jnp.zeros((HEAD_DIM, lanes), F32))


def _join(chains):
    return tuple(jnp.concatenate(parts, axis=1) for parts in zip(*chains))


def _split(joined, n):
    w = joined[0].shape[1] // n
    return [tuple(x[:, r * w:(r + 1) * w] for x in joined) for r in range(n)]


def _nsa_prompt_kernel(qt_ref, gate_ref, kc_ref, vct_ref, ks_ref, vst_ref, kw_ref, vwt_ref, slope_ref,
                       o_ref, selneg_ref, *, nb):
    g = pl.program_id(1)
    i = pl.program_id(2)
    tq, tk = NSA_TQ, NSA_TK
    lanes = NSA_GROUP * tq
    blocks_per_chunk = tk // CMP_BLOCK
    qt = jnp.concatenate([qt_ref[r * HEAD_DIM:(r + 1) * HEAD_DIM, :] for r in range(NSA_GROUP)], axis=1)
    qs = qt * SCALE
    rowgrp = _iota((NSA_KV * HEAD_DIM, 1), 0) // HEAD_DIM
    qpad = jnp.where(rowgrp == g, jnp.concatenate([qs] * NSA_KV, axis=0), 0.0).astype(BF16)
    slope = slope_ref[...]
    t_row = i * tq + _iota((1, lanes), 1) % tq
    tf = t_row.astype(F32)

    cend = _iota((nb, 1), 0) * CMP_BLOCK + (CMP_BLOCK - 1)
    sc = _dot(kc_ref[...], qs, HI) - slope * (tf - cend.astype(F32))
    vis = cend <= t_row
    sc = jnp.where(vis, sc, NEG)
    e = jnp.where(vis, jnp.exp(sc - jnp.max(sc, axis=0, keepdims=True)), 0.0)
    den = jnp.sum(e, axis=0, keepdims=True)
    p = e * jnp.where(den > 0.0, 1.0 / den, 0.0)
    o_cmp = _dot(vct_ref[...], p, HI)

    imp = p[:, 0:tq]
    for r in range(1, NSA_GROUP):
        imp = imp + p[:, r * tq:(r + 1) * tq]
    j = _iota((nb, tq), 0)
    own = (i * tq + _iota((nb, tq), 1)) // CMP_BLOCK
    forced = (j == 0) | (j == own) | (j == own - 1)
    score = jnp.where(forced, FORCE, imp)
    score = jnp.where(j <= own, score, -1.0)
    rank = jnp.zeros((nb, tq), F32)
    for m in range(nb):
        sm = score[m:m + 1, :]
        beats = (sm > score) | ((sm == score) & (j > m))
        rank = rank + jnp.where(beats, 1.0, 0.0)
    selneg = jnp.where((rank < float(min(SEL_TOPN, nb))) & (score >= 0.0), 0.0, MASK_BIAS)
    for c in range(nb // blocks_per_chunk):
        selneg_ref[c] = jnp.concatenate([selneg[c * blocks_per_chunk:(c + 1) * blocks_per_chunk],
                                         jnp.zeros((8 - blocks_per_chunk, tq), F32)], axis=0)

    sub = _iota((tk, 1), 0)
    col = _iota((tk, AUG), 1)
    aug = jnp.where(col < blocks_per_chunk, jnp.where(sub // CMP_BLOCK == col, 1.0, 0.0),
                    jnp.where((col >= 8) & (col < 11), sub.astype(F32), 0.0)).astype(BF16)
    slope8 = _slope_rows(slope, lanes)
    cd = (i * tq) // tk
    rel = (i * tq - cd * tk) + _iota((1, lanes), 1) % tq
    zeros8 = jnp.zeros((8, lanes), F32)
    causal = jnp.where(sub <= rel, 0.0, MASK_BIAS)
    win_tail = jnp.where((sub > rel) & (cd >= 2), 0.0, MASK_BIAS)

    def scores(k_ref, c, mask_rows):
        ks = pl.multiple_of(c * tk, tk)
        bias_rows = jnp.concatenate([mask_rows, slope8], axis=0).astype(BF16)
        return _dot(k_ref[pl.ds(ks, tk), :], qpad) + _dot(aug, bias_rows)

    def update(carry, s, vt_ref, c):
        ks = pl.multiple_of(c * tk, tk)
        return _biased_update(carry, s, slope * ((c * tk).astype(F32) - tf), vt_ref[:, pl.ds(ks, tk)].astype(BF16))

    def sel_rows(c):
        return jnp.concatenate([selneg_ref[c]] * NSA_GROUP, axis=1)

    init = _softmax_init(lanes)
    carry = update(init, scores(ks_ref, cd, sel_rows(cd)) + causal, vst_ref, cd)

    def slc_body(c, state):
        carry, s = state
        nxt = jnp.minimum(c + 1, jnp.maximum(cd - 1, 0))
        s_next = scores(ks_ref, nxt, sel_rows(nxt))
        return update(carry, s, vst_ref, c), s_next

    (_, l_s, acc_s), _ = lax.fori_loop(0, cd, slc_body, (carry, scores(ks_ref, 0, sel_rows(0))))

    c1, c2 = jnp.maximum(cd - 1, 0), jnp.maximum(cd - 2, 0)
    s0 = scores(kw_ref, cd, zeros8) + causal
    s1 = scores(kw_ref, c1, jnp.where(cd >= 1, zeros8, MASK_BIAS))
    s2 = scores(kw_ref, c2, zeros8) + win_tail
    _, l_w, acc_w = update(update(update(init, s0, vwt_ref, cd), s1, vwt_ref, c1), s2, vwt_ref, c2)

    gates = []
    for c in range(3):
        row = jnp.concatenate([gate_ref[c * NSA_GROUP + r:c * NSA_GROUP + r + 1, :] for r in range(NSA_GROUP)],
                              axis=1)
        gates.append(1.0 / (1.0 + jnp.exp(-row)))
    o = gates[0] * o_cmp + gates[1] * (acc_s / l_s) + gates[2] * (acc_w / l_w)
    for r in range(NSA_GROUP):
        o_ref[r * HEAD_DIM:(r + 1) * HEAD_DIM, :] = o[:, r * tq:(r + 1) * tq].astype(o_ref.dtype)


def nsa_prompt(qa_t, gate_t, kc, vct, ks, kvs_t, kw, kvw_t, slopes):
    bsz, _, seq = qa_t.shape
    nb = kc.shape[2]
    tq = NSA_TQ
    lanes = NSA_GROUP * tq
    gw = NSA_GROUP * HEAD_DIM
    assert seq % NSA_TK == 0 and WINDOW == 2 * NSA_TK and NSA_TK % NSA_TQ == 0
    return pl.pallas_call(
        functools.partial(_nsa_prompt_kernel, nb=nb),
        grid=(bsz, NSA_KV, seq // tq),
        in_specs=[pl.BlockSpec((None, gw, tq), lambda b, g, i: (b, g, i)),
                  pl.BlockSpec((None, 16, tq), lambda b, g, i: (b, g, i)),
                  pl.BlockSpec((None, None, nb, HEAD_DIM), lambda b, g, i: (b, g, 0, 0)),
                  pl.BlockSpec((None, None, HEAD_DIM, nb), lambda b, g, i: (b, g, 0, 0)),
                  pl.BlockSpec((None, seq, NSA_KV * HEAD_DIM), lambda b, g, i: (b, 0, 0)),
                  pl.BlockSpec((None, HEAD_DIM, seq), lambda b, g, i: (b, NSA_KV + g, 0)),
                  pl.BlockSpec((None, seq, NSA_KV * HEAD_DIM), lambda b, g, i: (b, 0, 0)),
                  pl.BlockSpec((None, HEAD_DIM, seq), lambda b, g, i: (b, NSA_KV + g, 0)),
                  pl.BlockSpec((None, 1, lanes), lambda b, g, i: (g, 0, 0))],
        out_specs=pl.BlockSpec((None, gw, tq), lambda b, g, i: (b, g, i)),
        out_shape=jax.ShapeDtypeStruct((bsz, NSA_HEADS * HEAD_DIM, seq), BF16),
        scratch_shapes=[pltpu.VMEM((nb * CMP_BLOCK // NSA_TK, 8, tq), F32)],
        compiler_params=_cparams(("parallel", "parallel", "arbitrary")), name="nsa_prompt",
    )(qa_t, gate_t, kc, vct, ks, kvs_t, kw, kvw_t, slopes)


def _sb_prompt_kernel(q_ref, k_ref, vt_ref, o_ref):
    i = pl.program_id(2)
    t = SB_T
    qpair = (q_ref[...].astype(F32) * SCALE).astype(BF16)
    rowh = _iota((2 * HEAD_DIM, 1), 0) // HEAD_DIM
    qcat = jnp.concatenate([jnp.where(rowh == hh, qpair, jnp.zeros_like(qpair)) for hh in range(2)], axis=1)
    th = t // 2
    after = jnp.where(_iota((th, th), 0) < _iota((th, th), 1), 1.0, 0.0).astype(BF16)
    vis = _iota((t, 1), 0) < _iota((1, 2 * t), 1) % t

    def logits(kb):
        return _dot(k_ref[pl.ds(pl.multiple_of(kb * t, t), t), :], qcat)

    def step(kb, carry, z, diagonal):
        acc, run = carry
        ks = pl.multiple_of(kb * t, t)
        sp = _softplus(z)
        ls = -sp
        if diagonal:
            ls = jnp.where(vis, ls, 0.0)
        hi = ls.astype(BF16)
        lo = (ls - hi.astype(F32)).astype(BF16)
        later = []
        for half in (1, 0):
            rows = slice(half * th, (half + 1) * th)
            later.insert(0, _dot(after, hi[rows]) + _dot(after, lo[rows]) + run)
            run = run + jnp.sum(ls[rows], axis=0, keepdims=True)
        w = jnp.exp(z - sp + jnp.concatenate(later, axis=0))
        if diagonal:
            w = jnp.where(vis, w, 0.0)
        w = w.astype(BF16)
        pv = [_dot(vt_ref[hh * HEAD_DIM:(hh + 1) * HEAD_DIM, pl.ds(ks, t)].astype(BF16), w[:, hh * t:(hh + 1) * t])
              for hh in range(2)]
        return acc + jnp.concatenate(pv, axis=1), run

    def body(st, state):
        carry, z = state
        kb = i - st
        z_next = logits(jnp.maximum(kb - 1, 0))
        return step(kb, carry, z, False), z_next

    carry = step(i, (jnp.zeros((HEAD_DIM, 2 * t), F32), jnp.zeros((1, 2 * t), F32)), logits(i), True)
    (acc, _), _ = lax.fori_loop(1, i + 1, body, (carry, logits(jnp.maximum(i - 1, 0))))
    for hh in range(2):
        o_ref[hh * HEAD_DIM:(hh + 1) * HEAD_DIM, :] = acc[:, hh * t:(hh + 1) * t].astype(o_ref.dtype)


def sb_prompt(qb_t, kb, kvb_t):
    bsz, width, seq = qb_t.shape
    pw = 2 * HEAD_DIM
    n_pairs = width // pw
    return pl.pallas_call(
        _sb_prompt_kernel,
        grid=(bsz, n_pairs, seq // SB_T),
        in_specs=[pl.BlockSpec((None, pw, SB_T), lambda b, p, i: (b, p, i)),
                  pl.BlockSpec((None, seq, pw), lambda b, p, i: (b, 0, p)),
                  pl.BlockSpec((None, pw, seq), lambda b, p, i: (b, n_pairs + p, 0))],
        out_specs=pl.BlockSpec((None, pw, SB_T), lambda b, p, i: (b, p, i)),
        out_shape=jax.ShapeDtypeStruct((bsz, width, seq), BF16),
        compiler_params=_cparams(("parallel", "parallel", "arbitrary")), name="sb_prompt",
    )(qb_t, kb, kvb_t)


def _block_mean_kernel(k_ref, o_ref):
    o_ref[...] = jnp.sum(k_ref[...], axis=0, keepdims=True) * (1.0 / MOBA_BLOCK)


def moba_block_means(k, bsz, seq):
    nblk = seq // MOBA_BLOCK
    width = k.shape[1]
    return pl.pallas_call(
        _block_mean_kernel,
        grid=(bsz, nblk),
        in_specs=[pl.BlockSpec((MOBA_BLOCK, width), lambda b, n: (b * nblk + n, 0))],
        out_specs=pl.BlockSpec((None, None, 1, width), lambda b, n: (b, n, 0, 0)),
        out_shape=jax.ShapeDtypeStruct((bsz, nblk, 1, width), F32),
        compiler_params=_cparams(("parallel", "parallel")), name="moba_block_means",
    )(k)


def _moba_prompt_kernel(qt_ref, means_ref, k_ref, vt_ref, slope_ref, o_ref, selneg_ref, *, nblk):
    g = pl.program_id(1)
    i = pl.program_id(2)
    tq = MOBA_BLOCK
    lanes = MOBA_GROUP * tq
    qt = jnp.concatenate([qt_ref[r * HEAD_DIM:(r + 1) * HEAD_DIM, :] for r in range(MOBA_GROUP)], axis=1)
    qs = qt * SCALE
    rowgrp = _iota((MOBA_KV * HEAD_DIM, 1), 0) // HEAD_DIM
    qpad = jnp.where(rowgrp == g, jnp.concatenate([qs] * MOBA_KV, axis=0), 0.0).astype(BF16)
    slope = slope_ref[...]
    t_row = i * tq + _iota((1, lanes), 1) % tq
    tf = t_row.astype(F32)

    gsc = _dot(means_ref[...], qt, HI)
    n_idx = _iota((nblk, lanes), 0)
    past = n_idx < i
    gsc = jnp.where(past, gsc, NEG)
    rank = jnp.zeros((nblk, lanes), F32)
    for m in range(nblk):
        gm = gsc[m:m + 1, :]
        beats = (gm > gsc) | ((gm == gsc) & (n_idx > m))
        rank = rank + jnp.where(beats, 1.0, 0.0)
    selneg_ref[...] = jnp.where((rank < float(min(MOBA_TOPK, nblk))) & past, 0.0, MASK_BIAS)

    sub = _iota((tq, 1), 0)
    ramp = slope * sub.astype(F32)

    def scores(n):
        return _dot(k_ref[pl.ds(pl.multiple_of(n * tq, tq), tq), :], qpad) + ramp

    def update(carry, s, n, mask_row):
        shift = slope * ((n * tq).astype(F32) - tf) + mask_row
        return _biased_update(carry, s, shift, vt_ref[:, pl.ds(pl.multiple_of(n * tq, tq), tq)].astype(BF16))

    causal = jnp.where(sub <= _iota((1, lanes), 1) % tq, 0.0, MASK_BIAS)
    carry = update(_softmax_init(lanes), scores(i) + causal, i, 0.0)

    def body(n, state):
        carry, s = state
        s_next = scores(jnp.minimum(n + 1, jnp.maximum(i - 1, 0)))
        return update(carry, s, n, selneg_ref[pl.ds(n, 1), :]), s_next

    (_, l, acc), _ = lax.fori_loop(0, i, body, (carry, scores(0)))
    o = acc / l
    for r in range(MOBA_GROUP):
        o_ref[r * HEAD_DIM:(r + 1) * HEAD_DIM, :] = o[:, r * tq:(r + 1) * tq].astype(o_ref.dtype)


def moba_prompt(q_t, means, k, kv_t, slopes):
    bsz, _, seq = q_t.shape
    nblk = seq // MOBA_BLOCK
    gw = MOBA_GROUP * HEAD_DIM
    lanes = MOBA_GROUP * MOBA_BLOCK
    return pl.pallas_call(
        functools.partial(_moba_prompt_kernel, nblk=nblk),
        grid=(bsz, MOBA_KV, nblk),
        in_specs=[pl.BlockSpec((None, gw, MOBA_BLOCK), lambda b, g, i: (b, g, i)),
                  pl.BlockSpec((None, None, nblk, HEAD_DIM), lambda b, g, i: (b, g, 0, 0)),
                  pl.BlockSpec((None, seq, MOBA_KV * HEAD_DIM), lambda b, g, i: (b, 0, 0)),
                  pl.BlockSpec((None, HEAD_DIM, seq), lambda b, g, i: (b, MOBA_KV + g, 0)),
                  pl.BlockSpec((None, 1, lanes), lambda b, g, i: (g, 0, 0))],
        out_specs=pl.BlockSpec((None, gw, MOBA_BLOCK), lambda b, g, i: (b, g, i)),
        out_shape=jax.ShapeDtypeStruct((bsz, MOBA_HEADS * HEAD_DIM, seq), BF16),
        scratch_shapes=[pltpu.VMEM((nblk, lanes), F32)],
        compiler_params=_cparams(("parallel", "parallel", "arbitrary")), name="moba_prompt",
    )(q_t, means, k, kv_t, slopes)


AB_COLS = dict(q_a=(0, 512), cmp=(512, 768), slc=(768, 1024), win=(1024, 1280), gate=(1280, 1304),
               q_b=(1304, 1816), kv_b=(1816, 2840))
AB_WIDTHS = (512, 256, 256, 256, 128, 512, 1024)


def _cols(w, name, lo=0, hi=None):
    a, b = AB_COLS[name]
    return w[:, a + lo:(b if hi is None else a + hi)]


def _ab_prompt_weights(w):
    w_std = jnp.concatenate([_cols(w, "cmp"), _cols(w, "slc", 0, 128), _cols(w, "win", 0, 128),
                             _cols(w, "kv_b", 0, 512)], axis=1).astype(BF16)
    wt = w.T
    g0 = AB_COLS["gate"][0]
    idx, keep = [], []
    for g in range(NSA_KV):
        for c in range(4):
            for r in range(NSA_GROUP):
                idx.append(g0 + (g * NSA_GROUP + r) * 3 + min(c, 2))
                keep.append(1.0 if c < 3 else 0.0)
    gate_rows = wt[jnp.asarray(idx)] * jnp.asarray(keep, F32)[:, None]
    rows = lambda name: wt[AB_COLS[name][0]:AB_COLS[name][1]]
    w_t = jnp.concatenate([rows("q_a"), rows("slc"), rows("win"), gate_rows, rows("q_b"), rows("kv_b")],
                          axis=0).astype(BF16)
    return w_std, w_t


AB_STD = ((0, 256, F32), (256, 128, BF16), (384, 128, BF16), (512, 512, BF16))
AB_TR = ((0, 512, F32), (512, 256, F32), (768, 256, F32), (1024, 32, F32), (1056, 512, BF16), (1568, 1024, F32))


def _group_slopes(n_heads, n_kv, tq):
    sl = jnp.asarray(_alibi_slopes(n_heads), F32).reshape(n_kv, n_heads // n_kv, 1)
    return jnp.broadcast_to(sl, (n_kv, n_heads // n_kv, tq)).reshape(n_kv, 1, (n_heads // n_kv) * tq)


def _cache_view(x_t, n_heads):
    bsz, _, seq = x_t.shape
    return jnp.transpose(x_t.reshape(bsz, 2, n_heads, HEAD_DIM, seq), (0, 4, 1, 2, 3))


def ab_prompt_mix(y, g_mix, w_std, w_t, c_pos, c_w1, c_w2, bsz, seq, tm):
    kv_cmp, ks, kw, kb, qa_t, kvs_t, kvw_t, gate_t, qb_t, kvb_t = rms_proj(
        y, g_mix, w_std, AB_STD, w_t, AB_TR, bsz, seq, tm)
    nb = seq // CMP_BLOCK

    x = jnp.transpose(kv_cmp.reshape(bsz, nb, CMP_BLOCK, 2, NSA_KV, HEAD_DIM), (3, 0, 1, 4, 2, 5))
    rows = bsz * nb * NSA_KV
    x = x.reshape(2, rows, CMP_BLOCK * HEAD_DIM)
    cmp = compress_blocks(x, c_pos.reshape(2, 1, CMP_BLOCK * HEAD_DIM), c_w1.astype(BF16), c_w2.astype(BF16),
                          min(rows, 256))
    cmp = cmp.reshape(2, bsz, nb, NSA_KV, HEAD_DIM)
    p_cmp = jnp.transpose(cmp, (1, 2, 0, 3, 4))
    kc = jnp.transpose(cmp[0], (0, 2, 1, 3))
    vct = jnp.transpose(cmp[1], (0, 2, 3, 1))

    kw_full = NSA_KV * HEAD_DIM
    oa_t = nsa_prompt(qa_t, gate_t, kc, vct, ks.reshape(bsz, seq, kw_full), kvs_t, kw.reshape(bsz, seq, kw_full),
                      kvw_t, _group_slopes(NSA_HEADS, NSA_KV, NSA_TQ))
    ob_t = sb_prompt(qb_t, kb.reshape(bsz, seq, SB_HEADS * HEAD_DIM), kvb_t)
    wlen = min(WINDOW, seq)
    return (jnp.concatenate([oa_t, ob_t], axis=1), _cache_view(kvs_t, NSA_KV), p_cmp,
            _cache_view(kvw_t[:, :, seq - wlen:], NSA_KV),
            _cache_view(kvb_t, SB_HEADS))


C_STD = ((0, 256, F32), (0, 256, BF16))
C_TR = ((0, 1024, F32), (1024, 512, F32))


def c_prompt_mix(y, g_mix, w_std, w_t, bsz, seq, tm):
    k32, k16, q_t, kv_t = rms_proj(y, g_mix, w_std, C_STD, w_t, C_TR, bsz, seq, tm)
    nblk = seq // MOBA_BLOCK
    means = moba_block_means(k32, bsz, seq).reshape(bsz, nblk, MOBA_KV, HEAD_DIM)
    means = jnp.transpose(means, (0, 2, 1, 3))
    o_t = moba_prompt(q_t, means, k16.reshape(bsz, seq, MOBA_KV * HEAD_DIM), kv_t,
                      _group_slopes(MOBA_HEADS, MOBA_KV, MOBA_BLOCK))
    return o_t, _cache_view(kv_t, MOBA_KV)


PAGES_PER_STEP = 8
N_SEL_OLD = SEL_TOPN - 1
BIG_IDX = 1 << 30


def _head_slopes(n_heads):
    return jnp.broadcast_to(jnp.asarray(_alibi_slopes(n_heads), F32).reshape(n_heads, 1), (n_heads, 128))


def _fold_heads(wide, n_groups, heads_per_group):
    n_heads = n_groups * heads_per_group
    hrow = _iota((n_heads, HEAD_DIM), 0) // heads_per_group
    out = jnp.zeros((n_heads, HEAD_DIM), F32)
    for g in range(n_groups):
        out = out + jnp.where(hrow == g, wide[:, g * HEAD_DIM:(g + 1) * HEAD_DIM], 0.0)
    return out


def _nsa_select_kernel(pt_ref, qm_ref, slope_ref, ccmp_ref, ocmp_ref, idx_ref, buf_ref, sem, *, n_pages):
    b = pl.program_id(0)
    row_copy = lambda p, page: pltpu.make_async_copy(ccmp_ref.at[pl.ds(page, 1), :], buf_ref.at[pl.ds(p, 1), :], sem)

    def issue(p, c):
        row_copy(p, pt_ref[b, p]).start()
        return c

    def drain(p, c):
        row_copy(p, 0).wait()
        return c

    lax.fori_loop(0, n_pages, issue, 0)
    lax.fori_loop(0, n_pages, drain, 0)

    buf = buf_ref[...]
    slope = slope_ref[:, 0:1]
    qpos = float(n_pages * PAGE_SIZE)
    page = _iota((1, n_pages), 1)
    st = _nt(qm_ref[...], buf, HI) * SCALE
    halves = []
    for blk in range(2):
        cend = (page * PAGE_SIZE + blk * CMP_BLOCK + (CMP_BLOCK - 1)).astype(F32)
        halves.append(st[blk * NSA_HEADS:(blk + 1) * NSA_HEADS] - slope * (qpos - cend))
    mx = jnp.maximum(jnp.max(halves[0], axis=1, keepdims=True), jnp.max(halves[1], axis=1, keepdims=True))
    e = [jnp.exp(hv - mx) for hv in halves]
    inv = 1.0 / (jnp.sum(e[0], axis=1, keepdims=True) + jnp.sum(e[1], axis=1, keepdims=True))
    p = [ev * inv for ev in e]
    v_lo = NSA_KV * HEAD_DIM
    ocmp_ref[...] = _dot(p[0], buf[:, v_lo:2 * v_lo], HI) + _dot(p[1], buf[:, 3 * v_lo:4 * v_lo], HI)

    hrow = _iota((NSA_HEADS, n_pages), 0) // NSA_GROUP
    jidx = (2 * _iota((2, n_pages), 1) + _iota((2, n_pages), 0)).astype(F32)
    lane = _iota((1, 128), 1)
    n_blk = 2 * n_pages
    rows = []
    for g in range(NSA_KV):
        imp = jnp.concatenate([jnp.sum(jnp.where(hrow == g, pv, 0.0), axis=0, keepdims=True) for pv in p], axis=0)
        cand = jnp.where((jidx == 0.0) | (jidx == float(n_blk - 1)), -1.0, imp)
        vec = jnp.where(lane == N_SEL_OLD - 1, float(n_blk - 1), 0.0)
        for r in range(N_SEL_OLD - 2):
            best = jnp.max(jnp.max(cand, axis=1, keepdims=True), axis=0, keepdims=True)
            hit = jnp.where(cand == best, jidx, float(BIG_IDX))
            pick = jnp.min(jnp.min(hit, axis=1, keepdims=True), axis=0, keepdims=True)
            vec = jnp.where(lane == r, pick, vec)
            cand = jnp.where(jidx == pick, -2.0, cand)
        rows.append(vec)
    idx_ref[...] = jnp.concatenate(rows + [jnp.zeros((8 - NSA_KV, 128), F32)], axis=0).astype(jnp.int32)


def nsa_sample_select(page_table, qm_cmp, c_cmp):
    bsz, n_pages = page_table.shape
    width = c_cmp.shape[1]
    assert 2 * n_pages - 2 >= N_SEL_OLD - 2
    return pl.pallas_call(
        functools.partial(_nsa_select_kernel, n_pages=n_pages),
        grid_spec=pltpu.PrefetchScalarGridSpec(
            num_scalar_prefetch=1,
            grid=(bsz,),
            in_specs=[pl.BlockSpec((None, 2 * NSA_HEADS, width), lambda b, pt: (b, 0, 0)),
                      pl.BlockSpec((NSA_HEADS, 128), lambda b, pt: (0, 0)),
                      pl.BlockSpec(memory_space=pl.ANY)],
            out_specs=[pl.BlockSpec((None, NSA_HEADS, 128), lambda b, pt: (b, 0, 0)),
                       pl.BlockSpec((None, 8, 128), lambda b, pt: (b, 0, 0))],
            scratch_shapes=[pltpu.VMEM((n_pages, width), F32), pltpu.SemaphoreType.DMA(())]),
        out_shape=[jax.ShapeDtypeStruct((bsz, NSA_HEADS, 128), F32),
                   jax.ShapeDtypeStruct((bsz, 8, 128), jnp.int32)],
        compiler_params=_cparams(("arbitrary",)), name="nsa_sample_select",
    )(page_table, qm_cmp, _head_slopes(NSA_HEADS), c_cmp)


def _nsa_attend_kernel(pt_ref, idx_ref, q_ref, slope_ref, gate_ref, ocmp_ref, new_slc_ref, new_win_ref,
                       new_win_t_ref, cwin_ref, cslc_ref, o_ref, swin_ref, kbuf_ref, vbuf_ref, sem, *, n_pages):
    b = pl.program_id(0)
    gw = NSA_KV * HEAD_DIM

    def page_copies(g, r):
        page = pt_ref[b, idx_ref[b, g * N_SEL_OLD + r] // 2]
        dst = pl.ds(r * PAGE_SIZE, PAGE_SIZE)
        return (pltpu.make_async_copy(cslc_ref.at[page, pl.ds(g * HEAD_DIM, HEAD_DIM), :],
                                      kbuf_ref.at[g, :, dst], sem),
                pltpu.make_async_copy(cslc_ref.at[page, pl.ds(gw + g * HEAD_DIM, HEAD_DIM), :],
                                      vbuf_ref.at[g, :, dst], sem))

    for g in range(NSA_KV):
        for r in range(N_SEL_OLD):
            for cp in page_copies(g, r):
                cp.start()
    for g in range(NSA_KV):
        for r in range(N_SEL_OLD):
            for cp in page_copies(g, r):
                cp.wait()

    q = q_ref[...]
    slope = slope_ref[:, 0:1]
    qpos = n_pages * PAGE_SIZE
    hgrp = _iota((NSA_HEADS, 1), 0) // NSA_GROUP
    n_keys = N_SEL_OLD * PAGE_SIZE
    lane = _iota((1, n_keys), 1)
    slot = lane // PAGE_SIZE
    tok = lane % PAGE_SIZE

    def per_group_rows(row):
        k = jnp.where(hgrp == 0, row[:, 0:HEAD_DIM], row[:, HEAD_DIM:2 * HEAD_DIM])
        v = jnp.where(hgrp == 0, row[:, gw:gw + HEAD_DIM], row[:, gw + HEAD_DIM:gw + 2 * HEAD_DIM])
        return k, v

    def finish(s, s_new, pv, v_new):
        mx = jnp.maximum(jnp.max(s, axis=1, keepdims=True), s_new)
        e = jnp.exp(s - mx)
        e_new = jnp.exp(s_new - mx)
        return (pv(e) + e_new * v_new) / (jnp.sum(e, axis=1, keepdims=True) + e_new)

    s_slc = jnp.zeros((NSA_HEADS, n_keys), F32)
    kpos = jnp.zeros((NSA_HEADS, n_keys), jnp.int32)
    valid = jnp.zeros((NSA_HEADS, n_keys), jnp.int32)
    for g in range(NSA_KV):
        s_slc = s_slc + _dot(jnp.where(hgrp == g, q, 0.0), kbuf_ref[g], HI)
        page_pos = jnp.zeros((1, n_keys), jnp.int32)
        half = jnp.zeros((1, n_keys), jnp.int32)
        for r in range(N_SEL_OLD):
            j = idx_ref[b, g * N_SEL_OLD + r]
            page_pos = jnp.where(slot == r, (j // 2) * PAGE_SIZE, page_pos)
            half = jnp.where(slot == r, j % 2, half)
        kpos = jnp.where(hgrp == g, page_pos + tok, kpos)
        valid = jnp.where(hgrp == g, jnp.where((tok // CMP_BLOCK) == half, 1, 0), valid)
    s_slc = jnp.where(valid > 0, s_slc * SCALE - slope * (qpos - kpos).astype(F32), NEG)
    k_new, v_new = per_group_rows(new_slc_ref[...])
    s_new = jnp.sum(q * k_new, axis=1, keepdims=True) * SCALE

    def pv_slc(e):
        return sum(_nt(jnp.where(hgrp == g, e, 0.0), vbuf_ref[g], HI) for g in range(NSA_KV))

    o_slc = finish(s_slc, s_new, pv_slc, v_new)

    cw = cwin_ref[...]
    wlen = cw.shape[1]
    wl = _iota((1, wlen), 1)
    q_groups = jnp.concatenate([jnp.where(hgrp == g, q, 0.0) for g in range(NSA_KV)], axis=1)
    s_w = _dot(q_groups, cw[0:gw], HI) * SCALE - slope * (wlen - wl).astype(F32)
    s_w = jnp.where(wl >= 1, s_w, NEG)
    k_new, v_new = per_group_rows(new_win_ref[...])
    s_new = jnp.sum(q * k_new, axis=1, keepdims=True) * SCALE
    o_win = finish(s_w, s_new, lambda e: _fold_heads(_nt(e, cw[gw:2 * gw], HI), NSA_KV, NSA_GROUP), v_new)

    new_t = new_win_t_ref[...]
    new_col = jnp.sum(jnp.where(_iota((1, new_t.shape[1]), 1) == b, new_t, 0.0), axis=1, keepdims=True)
    swin_ref[...] = jnp.where(wl == wlen - 1, new_col, pltpu.roll(cw, wlen - 1, 1))

    gate = 1.0 / (1.0 + jnp.exp(-gate_ref[...]))
    o_ref[...] = (gate[:, 0:1] * _fold_heads(ocmp_ref[...], NSA_KV, NSA_GROUP) + gate[:, 1:2] * o_slc
                  + gate[:, 2:3] * o_win)


def nsa_sample_attend(page_table, idx, q, gate, o_cmp, new_slc, new_win, new_win_t, c_win_t, c_slc_t):
    bsz, n_pages = page_table.shape
    width, wlen = c_win_t.shape[1:]
    per_b = lambda *shape: pl.BlockSpec((None,) + shape, lambda b, pt, ix: (b, 0, 0))
    n_keys = N_SEL_OLD * PAGE_SIZE
    return pl.pallas_call(
        functools.partial(_nsa_attend_kernel, n_pages=n_pages),
        grid_spec=pltpu.PrefetchScalarGridSpec(
            num_scalar_prefetch=2,
            grid=(bsz,),
            in_specs=[per_b(NSA_HEADS, HEAD_DIM),
                      pl.BlockSpec((NSA_HEADS, 128), lambda b, pt, ix: (0, 0)),
                      per_b(NSA_HEADS, 128), per_b(NSA_HEADS, 128), per_b(1, width), per_b(1, width),
                      pl.BlockSpec((None, width, bsz), lambda b, pt, ix: (0, 0, 0)),
                      per_b(width, wlen),
                      pl.BlockSpec(memory_space=pl.ANY)],
            out_specs=[per_b(NSA_HEADS, HEAD_DIM), per_b(width, wlen)],
            scratch_shapes=[pltpu.VMEM((NSA_KV, HEAD_DIM, n_keys), F32), pltpu.VMEM((NSA_KV, HEAD_DIM, n_keys), F32),
                            pltpu.SemaphoreType.DMA(())]),
        out_shape=[jax.ShapeDtypeStruct((bsz, NSA_HEADS, HEAD_DIM), F32),
                   jax.ShapeDtypeStruct((bsz, width, wlen), F32)],
        compiler_params=_cparams(("arbitrary",)), name="nsa_sample_attend",
    )(page_table, idx, q, _head_slopes(NSA_HEADS), gate, o_cmp, new_slc, new_win, new_win_t, c_win_t, c_slc_t)


def _sb_sample_kernel(pt_ref, qm_ref, *refs):
    page_refs = refs[:PAGES_PER_STEP]
    o_ref, acc_ref, run_ref = refs[PAGES_PER_STEP:]
    s = pl.program_id(1)

    @pl.when(s == 0)
    def _():
        acc_ref[...] = jnp.zeros_like(acc_ref)
        run_ref[...] = jnp.zeros_like(run_ref)

    qm = qm_ref[...]
    width = SB_HEADS * HEAD_DIM
    tk = PAGE_SIZE
    upper = jnp.where(_iota((tk, tk), 0) > _iota((tk, tk), 1), 1.0, 0.0).astype(BF16)
    acc = acc_ref[...]
    run = run_ref[:, 0:1]
    for page_ref in page_refs:
        z = _dot(qm, page_ref[0:width, :].astype(BF16))
        sp = _softplus(z)
        ls = -sp
        hi = ls.astype(BF16)
        lo = (ls - hi.astype(F32)).astype(BF16)
        later = _dot(hi, upper) + _dot(lo, upper) + run
        w = jnp.exp(z - sp + later)
        acc = acc + _nt(w.astype(BF16), page_ref[width:2 * width, :].astype(BF16))
        run = run + jnp.sum(ls, axis=1, keepdims=True)
    acc_ref[...] = acc
    run_ref[...] = jnp.broadcast_to(run, run_ref.shape)

    @pl.when(s == pl.num_programs(1) - 1)
    def _():
        own = _iota((SB_HEADS, width), 1) // HEAD_DIM == _iota((SB_HEADS, width), 0)
        o_ref[...] = jnp.sum(jnp.where(own, acc, 0.0), axis=0, keepdims=True)


def sb_sample(page_table, qm_bf16, c_sb_t):
    bsz, n_pages = page_table.shape
    width2 = c_sb_t.shape[1]
    assert n_pages % PAGES_PER_STEP == 0

    def page_spec(u):
        return pl.BlockSpec((None, width2, PAGE_SIZE),
                            lambda b, s, pt: (pt[b, n_pages - 1 - (s * PAGES_PER_STEP + u)], 0, 0))

    return pl.pallas_call(
        _sb_sample_kernel,
        grid_spec=pltpu.PrefetchScalarGridSpec(
            num_scalar_prefetch=1,
            grid=(bsz, n_pages // PAGES_PER_STEP),
            in_specs=[pl.BlockSpec((None, SB_HEADS, width2 // 2), lambda b, s, pt: (b, 0, 0))]
                     + [page_spec(u) for u in range(PAGES_PER_STEP)],
            out_specs=pl.BlockSpec((None, 1, width2 // 2), lambda b, s, pt: (b, 0, 0)),
            scratch_shapes=[pltpu.VMEM((SB_HEADS, width2 // 2), F32), pltpu.VMEM((SB_HEADS, 128), F32)]),
        out_shape=jax.ShapeDtypeStruct((bsz, 1, width2 // 2), F32),
        compiler_params=_cparams(("parallel", "arbitrary")), name="sb_sample",
    )(page_table, qm_bf16, *([c_sb_t] * PAGES_PER_STEP))


def _moba_sample_kernel(pt_ref, qm_ref, slope_ref, new_ref, *refs, n_pages):
    page_refs = refs[:PAGES_PER_STEP]
    o_ref, obuf_ref, m_ref, l_ref, means_ref = refs[PAGES_PER_STEP:]
    s = pl.program_id(1)
    width = MOBA_KV * HEAD_DIM
    qm = qm_ref[...]
    qb = (qm * SCALE).astype(BF16)
    slope = slope_ref[:, 0:1]
    qpos = n_pages * PAGE_SIZE
    lane = _iota((1, PAGE_SIZE), 1)
    plane = _iota((1, n_pages), 1)
    pages_per_block = MOBA_BLOCK // PAGE_SIZE

    @pl.when(s == 0)
    def _():
        m_ref[...] = jnp.zeros_like(m_ref)
        l_ref[...] = jnp.zeros_like(l_ref)
        means_ref[...] = jnp.zeros_like(means_ref)

    m_all = m_ref[...]
    l_all = l_ref[...]
    means = means_ref[...]
    keys = []
    for u, page_ref in enumerate(page_refs):
        pg = s * PAGES_PER_STEP + u
        kf = page_ref[0:width, :]
        sc = _dot(qb, kf.astype(BF16)) - slope * (qpos - (pg * PAGE_SIZE + lane)).astype(F32)
        mx = jnp.max(sc, axis=1, keepdims=True)
        e = jnp.exp(sc - mx)
        obuf_ref[pg] = _nt(e.astype(BF16), page_ref[width:2 * width, :].astype(BF16))
        m_all = jnp.where(plane == pg, mx, m_all)
        l_all = jnp.where(plane == pg, jnp.sum(e, axis=1, keepdims=True), l_all)
        keys.append(kf)
        if len(keys) == pages_per_block:
            mean = jnp.sum(sum(keys[1:], keys[0]), axis=1, keepdims=True) * (1.0 / MOBA_BLOCK)
            means = jnp.where(plane // pages_per_block == pg // pages_per_block, mean, means)
            keys = []
    m_ref[...] = m_all
    l_ref[...] = l_all
    means_ref[...] = means

    @pl.when(s == pl.num_programs(1) - 1)
    def _():
        gsc = _dot(qm, means, HI)
        cand = gsc
        blk_of_page = (plane // pages_per_block).astype(F32)
        sel = jnp.zeros(gsc.shape, jnp.bool_)
        for _ in range(MOBA_TOPK):
            best = jnp.max(cand, axis=1, keepdims=True)
            first = jnp.min(jnp.where(cand == best, blk_of_page, float(BIG_IDX)), axis=1, keepdims=True)
            hit = blk_of_page == first
            sel = sel | hit
            cand = jnp.where(hit, NEG, cand)
        new = new_ref[...]
        s_new = jnp.sum(qm * new[:, 0:width], axis=1, keepdims=True) * SCALE
        top = jnp.maximum(jnp.max(jnp.where(sel, m_all, NEG), axis=1, keepdims=True), s_new)
        coef = jnp.where(sel, jnp.exp(m_all - top), 0.0)
        e_new = jnp.exp(s_new - top)
        den = jnp.sum(coef * l_all, axis=1, keepdims=True) + e_new
        wide = e_new * new[:, width:2 * width]
        for p in range(n_pages):
            wide = wide + coef[:, p:p + 1] * obuf_ref[p]
        o_ref[...] = _fold_heads(wide, MOBA_KV, MOBA_GROUP) / den


def moba_sample(page_table, qm, new_kv, c_moba_t):
    bsz, n_pages = page_table.shape
    width2 = c_moba_t.shape[1]
    assert n_pages % PAGES_PER_STEP == 0 and n_pages * PAGE_SIZE >= MOBA_TOPK * MOBA_BLOCK

    def page_spec(u):
        return pl.BlockSpec((None, width2, PAGE_SIZE), lambda b, s, pt: (pt[b, s * PAGES_PER_STEP + u], 0, 0))

    return pl.pallas_call(
        functools.partial(_moba_sample_kernel, n_pages=n_pages),
        grid_spec=pltpu.PrefetchScalarGridSpec(
            num_scalar_prefetch=1,
            grid=(bsz, n_pages // PAGES_PER_STEP),
            in_specs=[pl.BlockSpec((None, MOBA_HEADS, width2 // 2), lambda b, s, pt: (b, 0, 0)),
                      pl.BlockSpec((MOBA_HEADS, 128), lambda b, s, pt: (0, 0)),
                      pl.BlockSpec((None, 1, width2), lambda b, s, pt: (b, 0, 0))]
                     + [page_spec(u) for u in range(PAGES_PER_STEP)],
            out_specs=pl.BlockSpec((None, MOBA_HEADS, HEAD_DIM), lambda b, s, pt: (b, 0, 0)),
            scratch_shapes=[pltpu.VMEM((n_pages, MOBA_HEADS, width2 // 2), F32),
                            pltpu.VMEM((MOBA_HEADS, n_pages), F32),
                            pltpu.VMEM((MOBA_HEADS, n_pages), F32),
                            pltpu.VMEM((width2 // 2, n_pages), F32)]),
        out_shape=jax.ShapeDtypeStruct((bsz, MOBA_HEADS, HEAD_DIM), F32),
        compiler_params=_cparams(("parallel", "arbitrary")), name="moba_sample",
    )(page_table, qm, _head_slopes(MOBA_HEADS), new_kv, *([c_moba_t] * PAGES_PER_STEP))


def _block_diag_query(q, n_groups):
    bsz, n_heads, d = q.shape
    onehot = (jnp.arange(n_heads)[:, None] // (n_heads // n_groups) == jnp.arange(n_groups)[None, :]).astype(q.dtype)
    return (q[:, :, None, :] * onehot[None, :, :, None]).reshape(bsz, n_heads, n_groups * d)


def _pages_feature_major(cache):
    n_phys, page = cache.shape[:2]
    return jnp.transpose(cache, (0, 2, 3, 4, 1)).reshape(n_phys, -1, page)


def _ab_sample_weights(w):
    g0, g1 = AB_COLS["gate"]
    w_std = jnp.concatenate([w[:, :g0], jnp.pad(w[:, g0:g1], ((0, 0), (0, 128 - N_GATE))), w[:, g1:]],
                            axis=1).astype(BF16)
    w_t = w.T[AB_COLS["win"][0]:AB_COLS["win"][1]].astype(BF16)
    return w_std, w_t


def ab_sample_mix(y, g_mix, w_std, w_t, c_slc, c_cmp, c_win, c_sb, page_table):
    bsz = y.shape[0]
    n_phys = c_slc.shape[0]
    std = []
    off = 0
    for wd in AB_WIDTHS:
        std.append((off, wd, F32))
        off += wd
    width = 2 * NSA_KV * HEAD_DIM
    q_a, kv_cmp, kv_slc, kv_win, gate, q_b, kv_b, kvw_t = rms_proj(y, g_mix, w_std, std, w_t, ((0, width, F32),),
                                                                   1, bsz, bsz)
    q_a = q_a.reshape(bsz, NSA_HEADS, HEAD_DIM)
    qm_kv = jnp.pad(_block_diag_query(q_a, NSA_KV), ((0, 0), (0, 0), (0, NSA_KV * HEAD_DIM)))
    zeros = jnp.zeros_like(qm_kv)
    qm_cmp = jnp.concatenate([jnp.concatenate([qm_kv, zeros], -1), jnp.concatenate([zeros, qm_kv], -1)], axis=1)
    o_cmp, idx = nsa_sample_select(page_table, qm_cmp, c_cmp.reshape(n_phys, -1))
    gate3 = jnp.pad(gate[:, :N_GATE].reshape(bsz, NSA_HEADS, 3), ((0, 0), (0, 0), (0, 125)))
    c_win_t = jnp.transpose(c_win, (0, 2, 3, 4, 1)).reshape(bsz, width, -1)
    o_a, s_win_t = nsa_sample_attend(page_table, idx[:, :NSA_KV, :N_SEL_OLD].reshape(bsz, -1), q_a, gate3, o_cmp,
                                     kv_slc.reshape(bsz, 1, width), kv_win.reshape(bsz, 1, width), kvw_t,
                                     c_win_t, _pages_feature_major(c_slc))
    qm_sb = (_block_diag_query(q_b.reshape(bsz, SB_HEADS, HEAD_DIM), SB_HEADS) * SCALE).astype(BF16)
    o_b = sb_sample(page_table, qm_sb, _pages_feature_major(c_sb))
    o_ts = jnp.concatenate([o_a.reshape(bsz, -1), o_b.reshape(bsz, -1)], axis=1).T[None].astype(BF16)
    shape5 = (bsz, 1, 2, NSA_KV, HEAD_DIM)
    s_win = jnp.transpose(s_win_t.reshape(bsz, 2, NSA_KV, HEAD_DIM, -1), (0, 4, 1, 2, 3))
    return (o_ts, kv_slc.reshape(shape5), kv_cmp.reshape(shape5), s_win,
            kv_b.reshape(bsz, 1, 2, SB_HEADS, HEAD_DIM))


def c_sample_mix(y, g_mix, w_std, c_moba, page_table):
    bsz = y.shape[0]
    q, kv = rms_proj(y, g_mix, w_std, ((0, 1024, F32), (1024, 512, F32)), None, (), 1, bsz, bsz)
    qm = _block_diag_query(q.reshape(bsz, MOBA_HEADS, HEAD_DIM), MOBA_KV)
    o = moba_sample(page_table, qm, kv.reshape(bsz, 1, -1), _pages_feature_major(c_moba))
    return o.reshape(bsz, -1).T[None].astype(BF16), kv.reshape(bsz, 1, 2, MOBA_KV, HEAD_DIM)


def kernel(x_prompt, x_sample, cache_nsa_slc, cache_nsa_cmp, cache_nsa_win, cache_sb, cache_moba, page_table,
           norm_mix, w_in_ab, w_out_ab, cmp_pos, cmp_w1, cmp_w2, w_in_c, w_out_c,
           norm_ffn, w_up, w_down, norm_final):
    bsz, seq, d = x_prompt.shape
    tm = 512
    w_up_b = w_up.astype(BF16)
    w_down_b = w_down.astype(BF16)
    dbsz = x_sample.shape[0]
    yp = x_prompt.reshape(bsz * seq, d)
    ys = x_sample.reshape(dbsz, d)
    w_oab = w_out_ab[0].astype(BF16)
    w_oc = w_out_c[0].astype(BF16)
    wc = w_in_c[0]
    k_lo, k_hi = MOBA_HEADS * HEAD_DIM, (MOBA_HEADS + MOBA_KV) * HEAD_DIM

    ab_std, ab_t = _ab_prompt_weights(w_in_ab[0])
    o_ts, p_slc, p_cmp, p_win, p_sb = ab_prompt_mix(yp, norm_mix[0], ab_std, ab_t, cmp_pos[0], cmp_w1[0], cmp_w2[0],
                                                    bsz, seq, tm)
    yp = mix_mlp(yp, o_ts, w_oab, norm_ffn[0], w_up_b[0], w_down_b[0], norm_final, False, bsz, seq, tm)
    o_ts, p_moba = c_prompt_mix(yp, norm_mix[1], wc[:, k_lo:k_hi].astype(BF16), wc.T.astype(BF16), bsz, seq, tm)
    yp = mix_mlp(yp, o_ts, w_oc, norm_ffn[1], w_up_b[1], w_down_b[1], norm_final, True, bsz, seq, tm)

    abs_std, abs_t = _ab_sample_weights(w_in_ab[0])
    o_ts, s_slc, s_cmp, s_win, s_sb = ab_sample_mix(ys, norm_mix[0], abs_std, abs_t, cache_nsa_slc[0],
                                                    cache_nsa_cmp[0], cache_nsa_win[0], cache_sb[0], page_table)
    ys = mix_mlp(ys, o_ts, w_oab, norm_ffn[0], w_up_b[0], w_down_b[0], norm_final, False, 1, dbsz, dbsz)
    o_ts, s_moba = c_sample_mix(ys, norm_mix[1], wc.astype(BF16), cache_moba[0], page_table)
    ys = mix_mlp(ys, o_ts, w_oc, norm_ffn[1], w_up_b[1], w_down_b[1], norm_final, True, 1, dbsz, dbsz)

    return (yp.reshape(bsz, seq, d), ys.reshape(dbsz, 1, d), p_slc[None], p_cmp[None], p_win[None], p_sb[None],
            p_moba[None], s_slc[None], s_cmp[None], s_win[None], s_sb[None], s_moba[None])
```

```python
import functools

import jax
import jax.numpy as jnp
from jax import lax
from jax.experimental import pallas as pl
from jax.experimental.pallas import tpu as pltpu

HEAD_DIM = 64
NSA_HEADS = 8
NSA_KV = 2
NSA_GROUP = NSA_HEADS // NSA_KV
CMP_BLOCK = 64
SEL_TOPN = 16
WINDOW = 512
SB_HEADS = 8
MOBA_HEADS = 16
MOBA_KV = 4
MOBA_GROUP = MOBA_HEADS // MOBA_KV
MOBA_BLOCK = 256
MOBA_TOPK = 3
PAGE_SIZE = 128
EPS = 1e-6
NEG = -1e30
MASK_BIAS = -(2.0 ** 100)
FORCE = 1e4
SCALE = HEAD_DIM ** -0.5

F32 = jnp.float32
BF16 = jnp.bfloat16
HI = lax.Precision.HIGHEST

VMEM_LIMIT = 48 * 1024 * 1024

NSA_TQ = 128
NSA_TK = 256
SB_T = 256
SB_CUTOFF = -110.0
AUG = 16
N_GATE = 3 * NSA_HEADS


def _cparams(sem):
    return pltpu.CompilerParams(dimension_semantics=sem, vmem_limit_bytes=VMEM_LIMIT)


def _alibi_slopes(n):
    return [2.0 ** (-8.0 * (h + 1) / n) for h in range(n)]


def _nt(a, b, precision=None):
    return lax.dot_general(a, b, (((1,), (1,)), ((), ())), precision=precision, preferred_element_type=F32)


def _tn(a, b):
    return lax.dot_general(a, b, (((0,), (0,)), ((), ())), preferred_element_type=F32)


def _dot(a, b, precision=None):
    return jnp.dot(a, b, precision=precision, preferred_element_type=F32)


def _rms(x, g):
    return x * lax.rsqrt(jnp.mean(x * x, axis=-1, keepdims=True) + EPS) * g


def _softplus(z):
    return jnp.maximum(z, 0.0) + jnp.log(1.0 + jnp.exp(-jnp.abs(z)))


def _iota(shape, axis):
    return lax.broadcasted_iota(jnp.int32, shape, axis)


def _proj_kernel(*refs, std, tr):
    x_ref, g_ref = refs[:2]
    refs = refs[2:]
    ws_ref = wt_ref = None
    if std:
        ws_ref, refs = refs[0], refs[1:]
    if tr:
        wt_ref, refs = refs[0], refs[1:]
    h = _rms(x_ref[...], g_ref[...]).astype(BF16)
    k = 0
    for off, wd, _ in std:
        refs[k][...] = _dot(h, ws_ref[:, off:off + wd]).astype(refs[k].dtype)
        k += 1
    for off, rows, _ in tr:
        refs[k][...] = _nt(wt_ref[off:off + rows, :], h).astype(refs[k].dtype)
        k += 1


def rms_proj(x, g, w_std, std, w_t, tr, bsz, seq, tm):
    m, d = x.shape
    nj = seq // tm
    assert m == bsz * seq and seq % tm == 0
    const = lambda i: (0, 0)
    operands = [x, g.reshape(1, d)]
    in_specs = [pl.BlockSpec((tm, d), lambda i: (i, 0)), pl.BlockSpec((1, d), const)]
    if std:
        operands.append(w_std)
        in_specs.append(pl.BlockSpec(w_std.shape, const))
    if tr:
        operands.append(w_t)
        in_specs.append(pl.BlockSpec(w_t.shape, const))
    out_specs = ([pl.BlockSpec((tm, wd), lambda i: (i, 0)) for _, wd, _ in std]
                 + [pl.BlockSpec((None, rows, tm), lambda i: (i // nj, 0, i % nj)) for _, rows, _ in tr])
    out_shape = ([jax.ShapeDtypeStruct((m, wd), dt) for _, wd, dt in std]
                 + [jax.ShapeDtypeStruct((bsz, rows, seq), dt) for _, rows, dt in tr])
    return pl.pallas_call(
        functools.partial(_proj_kernel, std=tuple(std), tr=tuple(tr)),
        grid=(m // tm,), in_specs=in_specs, out_specs=out_specs, out_shape=out_shape,
        compiler_params=_cparams(("parallel",)), name="rms_proj",
    )(*operands)


def _mix_mlp_kernel(y_ref, o_ref, wo_ref, g_ref, wu_ref, wd_ref, gf_ref, out_ref, *, ff_chunk, final):
    y = y_ref[...] + _tn(o_ref[...], wo_ref[...])
    h = _rms(y, g_ref[...]).astype(BF16)
    acc = y
    for c in range(0, wu_ref.shape[1], ff_chunk):
        a = jnp.maximum(_dot(h, wu_ref[:, c:c + ff_chunk]), 0.0)
        acc = acc + _dot((a * a).astype(BF16), wd_ref[c:c + ff_chunk, :])
    if final:
        acc = _rms(acc, gf_ref[...])
    out_ref[...] = acc


def mix_mlp(y, o_t, w_out, g_ffn, w_up, w_down, g_final, final, bsz, seq, tm):
    m, d = y.shape
    f = w_up.shape[1]
    nj = seq // tm
    const = lambda i: (0, 0)
    return pl.pallas_call(
        functools.partial(_mix_mlp_kernel, ff_chunk=min(f, 1024), final=final),
        grid=(m // tm,),
        in_specs=[pl.BlockSpec((tm, d), lambda i: (i, 0)),
                  pl.BlockSpec((None, o_t.shape[1], tm), lambda i: (i // nj, 0, i % nj)),
                  pl.BlockSpec((d, d), const), pl.BlockSpec((1, d), const), pl.BlockSpec((d, f), const),
                  pl.BlockSpec((f, d), const), pl.BlockSpec((1, d), const)],
        out_specs=pl.BlockSpec((tm, d), lambda i: (i, 0)),
        out_shape=jax.ShapeDtypeStruct((m, d), F32),
        compiler_params=_cparams(("parallel",)), name="mix_mlp",
    )(y, o_t, w_out, g_ffn.reshape(1, d), w_up, w_down, g_final.reshape(1, d))


def _compress_kernel(x_ref, pos_ref, w1_ref, w2_ref, o_ref):
    x = (x_ref[...] + pos_ref[...]).astype(BF16)
    hid = _dot(x, w1_ref[...])
    hid = hid * (1.0 / (1.0 + jnp.exp(-hid)))
    o_ref[...] = _dot(hid.astype(BF16), w2_ref[...])


def compress_blocks(x, pos, w1_bf16, w2_bf16, tr):
    _, r, fdim = x.shape
    hdim = w1_bf16.shape[2]
    return pl.pallas_call(
        _compress_kernel,
        grid=(2, r // tr),
        in_specs=[pl.BlockSpec((None, tr, fdim), lambda k, i: (k, i, 0)),
                  pl.BlockSpec((None, 1, fdim), lambda k, i: (k, 0, 0)),
                  pl.BlockSpec((None, fdim, hdim), lambda k, i: (k, 0, 0)),
                  pl.BlockSpec((None, hdim, HEAD_DIM), lambda k, i: (k, 0, 0))],
        out_specs=pl.BlockSpec((None, tr, HEAD_DIM), lambda k, i: (k, i, 0)),
        out_shape=jax.ShapeDtypeStruct((2, r, HEAD_DIM), F32),
        compiler_params=_cparams(("parallel", "parallel")), name="nsa_compress",
    )(x, pos, w1_bf16, w2_bf16)


def _biased_update(carry, s, shift, vt):
    m, l, acc = carry
    m_new = jnp.maximum(m, jnp.max(s, axis=0, keepdims=True) + shift)
    alpha = jnp.exp(m - m_new)
    p = jnp.exp(s - (m_new - shift))
    l = alpha * l + jnp.sum(p, axis=0, keepdims=True)
    acc = alpha * acc + _dot(vt, p.astype(BF16))
    return m_new, l, acc


def _slope_rows(slope, lanes):
    p1 = slope.astype(BF16).astype(F32)
    p2 = (slope - p1).astype(BF16).astype(F32)
    p3 = slope - p1 - p2
    return jnp.concatenate([p1, p2, p3, jnp.zeros((5, lanes), F32)], axis=0)


def _softmax_init(lanes):
    return (jnp.full((1, lanes), NEG, F32), jnp.zeros((1, lanes), F32), jnp.zeros((HEAD_DIM, lanes), F32))


def _join(chains):
    return tuple(jnp.concatenate(parts, axis=1) for parts in zip(*chains))


def _split(joined, n):
    w = joined[0].shape[1] // n
    return [tuple(x[:, r * w:(r + 1) * w] for x in joined) for r in range(n)]


def _nsa_prompt_kernel(qt_ref, gate_ref, kc_ref, vct_ref, ks_ref, vst_ref, kw_ref, vwt_ref, slope_ref,
                       o_ref, selneg_ref, *, nb):
    g = pl.program_id(1)
    i = pl.program_id(2)
    tq, tk = NSA_TQ, NSA_TK
    lanes = NSA_GROUP * tq
    blocks_per_chunk = tk // CMP_BLOCK
    qt = jnp.concatenate([qt_ref[r * HEAD_DIM:(r + 1) * HEAD_DIM, :] for r in range(NSA_GROUP)], axis=1)
    qs = qt * SCALE
    rowgrp = _iota((NSA_KV * HEAD_DIM, 1), 0) // HEAD_DIM
    qpad = jnp.where(rowgrp == g, jnp.concatenate([qs] * NSA_KV, axis=0), 0.0).astype(BF16)
    slope = slope_ref[...]
    t_row = i * tq + _iota((1, lanes), 1) % tq
    tf = t_row.astype(F32)

    cend = _iota((nb, 1), 0) * CMP_BLOCK + (CMP_BLOCK - 1)
    sc = _dot(kc_ref[...], qs, HI) - slope * (tf - cend.astype(F32))
    vis = cend <= t_row
    sc = jnp.where(vis, sc, NEG)
    e = jnp.where(vis, jnp.exp(sc - jnp.max(sc, axis=0, keepdims=True)), 0.0)
    den = jnp.sum(e, axis=0, keepdims=True)
    p = e * jnp.where(den > 0.0, 1.0 / den, 0.0)
    o_cmp = _dot(vct_ref[...], p, HI)

    imp = p[:, 0:tq]
    for r in range(1, NSA_GROUP):
        imp = imp + p[:, r * tq:(r + 1) * tq]
    j = _iota((nb, tq), 0)
    own = (i * tq + _iota((nb, tq), 1)) // CMP_BLOCK
    forced = (j == 0) | (j == own) | (j == own - 1)
    score = jnp.where(forced, FORCE, imp)
    score = jnp.where(j <= own, score, -1.0)
    rank = jnp.zeros((nb, tq), F32)
    for m in range(nb):
        sm = score[m:m + 1, :]
        beats = (sm > score) | ((sm == score) & (j > m))
        rank = rank + jnp.where(beats, 1.0, 0.0)
    selneg = jnp.where((rank < float(min(SEL_TOPN, nb))) & (score >= 0.0), 0.0, MASK_BIAS)
    for c in range(nb // blocks_per_chunk):
        selneg_ref[c] = jnp.concatenate([selneg[c * blocks_per_chunk:(c + 1) * blocks_per_chunk],
                                         jnp.zeros((8 - blocks_per_chunk, tq), F32)], axis=0)

    sub = _iota((tk, 1), 0)
    col = _iota((tk, AUG), 1)
    aug = jnp.where(col < blocks_per_chunk, jnp.where(sub // CMP_BLOCK == col, 1.0, 0.0),
                    jnp.where((col >= 8) & (col < 11), sub.astype(F32), 0.0)).astype(BF16)
    slope8 = _slope_rows(slope, lanes)
    cd = (i * tq) // tk
    rel = (i * tq - cd * tk) + _iota((1, lanes), 1) % tq
    zeros8 = jnp.zeros((8, lanes), F32)
    causal = jnp.where(sub <= rel, 0.0, MASK_BIAS)
    win_tail = jnp.where((sub > rel) & (cd >= 2), 0.0, MASK_BIAS)

    def scores(k_ref, c, mask_rows):
        ks = pl.multiple_of(c * tk, tk)
        bias_rows = jnp.concatenate([mask_rows, slope8], axis=0).astype(BF16)
        return _dot(k_ref[pl.ds(ks, tk), :], qpad) + _dot(aug, bias_rows)

    def update(carry, s, vt_ref, c):
        ks = pl.multiple_of(c * tk, tk)
        return _biased_update(carry, s, slope * ((c * tk).astype(F32) - tf), vt_ref[:, pl.ds(ks, tk)].astype(BF16))

    def sel_rows(c):
        return jnp.concatenate([selneg_ref[c]] * NSA_GROUP, axis=1)

    init = _softmax_init(lanes)
    carry = update(init, scores(ks_ref, cd, sel_rows(cd)) + causal, vst_ref, cd)

    def slc_body(c, state):
        carry, s = state
        nxt = jnp.minimum(c + 1, jnp.maximum(cd - 1, 0))
        s_next = scores(ks_ref, nxt, sel_rows(nxt))
        return update(carry, s, vst_ref, c), s_next

    (_, l_s, acc_s), _ = lax.fori_loop(0, cd, slc_body, (carry, scores(ks_ref, 0, sel_rows(0))))

    c1, c2 = jnp.maximum(cd - 1, 0), jnp.maximum(cd - 2, 0)
    s0 = scores(kw_ref, cd, zeros8) + causal
    s1 = scores(kw_ref, c1, jnp.where(cd >= 1, zeros8, MASK_BIAS))
    s2 = scores(kw_ref, c2, zeros8) + win_tail
    _, l_w, acc_w = update(update(update(init, s0, vwt_ref, cd), s1, vwt_ref, c1), s2, vwt_ref, c2)

    gates = []
    for c in range(3):
        row = jnp.concatenate([gate_ref[c * NSA_GROUP + r:c * NSA_GROUP + r + 1, :] for r in range(NSA_GROUP)],
                              axis=1)
        gates.append(1.0 / (1.0 + jnp.exp(-row)))
    o = gates[0] * o_cmp + gates[1] * (acc_s / l_s) + gates[2] * (acc_w / l_w)
    for r in range(NSA_GROUP):
        o_ref[r * HEAD_DIM:(r + 1) * HEAD_DIM, :] = o[:, r * tq:(r + 1) * tq].astype(o_ref.dtype)


def nsa_prompt(qa_t, gate_t, kc, vct, ks, kvs_t, kw, kvw_t, slopes):
    bsz, _, seq = qa_t.shape
    nb = kc.shape[2]
    tq = NSA_TQ
    lanes = NSA_GROUP * tq
    gw = NSA_GROUP * HEAD_DIM
    assert seq % NSA_TK == 0 and WINDOW == 2 * NSA_TK and NSA_TK % NSA_TQ == 0
    return pl.pallas_call(
        functools.partial(_nsa_prompt_kernel, nb=nb),
        grid=(bsz, NSA_KV, seq // tq),
        in_specs=[pl.BlockSpec((None, gw, tq), lambda b, g, i: (b, g, i)),
                  pl.BlockSpec((None, 16, tq), lambda b, g, i: (b, g, i)),
                  pl.BlockSpec((None, None, nb, HEAD_DIM), lambda b, g, i: (b, g, 0, 0)),
                  pl.BlockSpec((None, None, HEAD_DIM, nb), lambda b, g, i: (b, g, 0, 0)),
                  pl.BlockSpec((None, seq, NSA_KV * HEAD_DIM), lambda b, g, i: (b, 0, 0)),
                  pl.BlockSpec((None, HEAD_DIM, seq), lambda b, g, i: (b, NSA_KV + g, 0)),
                  pl.BlockSpec((None, seq, NSA_KV * HEAD_DIM), lambda b, g, i: (b, 0, 0)),
                  pl.BlockSpec((None, HEAD_DIM, seq), lambda b, g, i: (b, NSA_KV + g, 0)),
                  pl.BlockSpec((None, 1, lanes), lambda b, g, i: (g, 0, 0))],
        out_specs=pl.BlockSpec((None, gw, tq), lambda b, g, i: (b, g, i)),
        out_shape=jax.ShapeDtypeStruct((bsz, NSA_HEADS * HEAD_DIM, seq), BF16),
        scratch_shapes=[pltpu.VMEM((nb * CMP_BLOCK // NSA_TK, 8, tq), F32)],
        compiler_params=_cparams(("parallel", "parallel", "arbitrary")), name="nsa_prompt",
    )(qa_t, gate_t, kc, vct, ks, kvs_t, kw, kvw_t, slopes)


def _sb_prompt_kernel(q_ref, k_ref, vt_ref, o_ref):
    i = pl.program_id(2)
    t = SB_T
    qpair = (q_ref[...].astype(F32) * SCALE).astype(BF16)
    rowh = _iota((2 * HEAD_DIM, 1), 0) // HEAD_DIM
    qcat = jnp.concatenate([jnp.where(rowh == hh, qpair, jnp.zeros_like(qpair)) for hh in range(2)], axis=1)
    th = t // 2
    after = jnp.where(_iota((th, th), 0) < _iota((th, th), 1), 1.0, 0.0).astype(BF16)
    vis = _iota((t, 1), 0) < _iota((1, 2 * t), 1) % t

    def logits(kb):
        return _dot(k_ref[pl.ds(pl.multiple_of(kb * t, t), t), :], qcat)

    def step(kb, carry, z, diagonal):
        acc, run = carry
        ks = pl.multiple_of(kb * t, t)
        sp = _softplus(z)
        ls = -sp
        if diagonal:
            ls = jnp.where(vis, ls, 0.0)
        hi = ls.astype(BF16)
        lo = (ls - hi.astype(F32)).astype(BF16)
        later = []
        for half in (1, 0):
            rows = slice(half * th, (half + 1) * th)
            later.insert(0, _dot(after, hi[rows]) + _dot(after, lo[rows]) + run)
            run = run + jnp.sum(ls[rows], axis=0, keepdims=True)
        w = jnp.exp(z - sp + jnp.concatenate(later, axis=0))
        if diagonal:
            w = jnp.where(vis, w, 0.0)
        w = w.astype(BF16)
        pv = [_dot(vt_ref[hh * HEAD_DIM:(hh + 1) * HEAD_DIM, pl.ds(ks, t)].astype(BF16), w[:, hh * t:(hh + 1) * t])
              for hh in range(2)]
        return acc + jnp.concatenate(pv, axis=1), run

    def body(state):
        st, carry, z = state
        kb = i - st
        z_next = logits(jnp.maximum(kb - 1, 0))
        return st + 1, step(kb, carry, z, False), z_next

    def more(state):
        st, (_, run), _ = state
        return (st <= i) & (jnp.max(run) > SB_CUTOFF)

    carry = step(i, (jnp.zeros((HEAD_DIM, 2 * t), F32), jnp.zeros((1, 2 * t), F32)), logits(i), True)
    _, (acc, _), _ = lax.while_loop(more, body, (jnp.int32(1), carry, logits(jnp.maximum(i - 1, 0))))
    for hh in range(2):
        o_ref[hh * HEAD_DIM:(hh + 1) * HEAD_DIM, :] = acc[:, hh * t:(hh + 1) * t].astype(o_ref.dtype)


def sb_prompt(qb_t, kb, kvb_t):
    bsz, width, seq = qb_t.shape
    pw = 2 * HEAD_DIM
    n_pairs = width // pw
    return pl.pallas_call(
        _sb_prompt_kernel,
        grid=(bsz, n_pairs, seq // SB_T),
        in_specs=[pl.BlockSpec((None, pw, SB_T), lambda b, p, i: (b, p, i)),
                  pl.BlockSpec((None, seq, pw), lambda b, p, i: (b, 0, p)),
                  pl.BlockSpec((None, pw, seq), lambda b, p, i: (b, n_pairs + p, 0))],
        out_specs=pl.BlockSpec((None, pw, SB_T), lambda b, p, i: (b, p, i)),
        out_shape=jax.ShapeDtypeStruct((bsz, width, seq), BF16),
        compiler_params=_cparams(("parallel", "parallel", "arbitrary")), name="sb_prompt",
    )(qb_t, kb, kvb_t)


def _block_mean_kernel(k_ref, o_ref):
    o_ref[...] = jnp.sum(k_ref[...], axis=0, keepdims=True) * (1.0 / MOBA_BLOCK)


def moba_block_means(k, bsz, seq):
    nblk = seq // MOBA_BLOCK
    width = k.shape[1]
    return pl.pallas_call(
        _block_mean_kernel,
        grid=(bsz, nblk),
        in_specs=[pl.BlockSpec((MOBA_BLOCK, width), lambda b, n: (b * nblk + n, 0))],
        out_specs=pl.BlockSpec((None, None, 1, width), lambda b, n: (b, n, 0, 0)),
        out_shape=jax.ShapeDtypeStruct((bsz, nblk, 1, width), F32),
        compiler_params=_cparams(("parallel", "parallel")), name="moba_block_means",
    )(k)


def _moba_prompt_kernel(qt_ref, means_ref, k_ref, vt_ref, slope_ref, o_ref, selneg_ref, *, nblk):
    g = pl.program_id(1)
    i = pl.program_id(2)
    tq = MOBA_BLOCK
    lanes = MOBA_GROUP * tq
    qt = jnp.concatenate([qt_ref[r * HEAD_DIM:(r + 1) * HEAD_DIM, :] for r in range(MOBA_GROUP)], axis=1)
    qs = qt * SCALE
    rowgrp = _iota((MOBA_KV * HEAD_DIM, 1), 0) // HEAD_DIM
    qpad = jnp.where(rowgrp == g, jnp.concatenate([qs] * MOBA_KV, axis=0), 0.0).astype(BF16)
    slope = slope_ref[...]
    t_row = i * tq + _iota((1, lanes), 1) % tq
    tf = t_row.astype(F32)

    gsc = _dot(means_ref[...], qt, HI)
    n_idx = _iota((nblk, lanes), 0)
    past = n_idx < i
    gsc = jnp.where(past, gsc, NEG)
    rank = jnp.zeros((nblk, lanes), F32)
    for m in range(nblk):
        gm = gsc[m:m + 1, :]
        beats = (gm > gsc) | ((gm == gsc) & (n_idx > m))
        rank = rank + jnp.where(beats, 1.0, 0.0)
    selneg_ref[...] = jnp.where((rank < float(min(MOBA_TOPK, nblk))) & past, 0.0, MASK_BIAS)

    sub = _iota((tq, 1), 0)
    ramp = slope * sub.astype(F32)

    def scores(n):
        return _dot(k_ref[pl.ds(pl.multiple_of(n * tq, tq), tq), :], qpad) + ramp

    def update(carry, s, n, mask_row):
        shift = slope * ((n * tq).astype(F32) - tf) + mask_row
        return _biased_update(carry, s, shift, vt_ref[:, pl.ds(pl.multiple_of(n * tq, tq), tq)].astype(BF16))

    causal = jnp.where(sub <= _iota((1, lanes), 1) % tq, 0.0, MASK_BIAS)
    carry = update(_softmax_init(lanes), scores(i) + causal, i, 0.0)

    def body(n, state):
        carry, s = state
        s_next = scores(jnp.minimum(n + 1, jnp.maximum(i - 1, 0)))
        return update(carry, s, n, selneg_ref[pl.ds(n, 1), :]), s_next

    (_, l, acc), _ = lax.fori_loop(0, i, body, (carry, scores(0)))
    o = acc / l
    for r in range(MOBA_GROUP):
        o_ref[r * HEAD_DIM:(r + 1) * HEAD_DIM, :] = o[:, r * tq:(r + 1) * tq].astype(o_ref.dtype)


def moba_prompt(q_t, means, k, kv_t, slopes):
    bsz, _, seq = q_t.shape
    nblk = seq // MOBA_BLOCK
    gw = MOBA_GROUP * HEAD_DIM
    lanes = MOBA_GROUP * MOBA_BLOCK
    return pl.pallas_call(
        functools.partial(_moba_prompt_kernel, nblk=nblk),
        grid=(bsz, MOBA_KV, nblk),
        in_specs=[pl.BlockSpec((None, gw, MOBA_BLOCK), lambda b, g, i: (b, g, i)),
                  pl.BlockSpec((None, None, nblk, HEAD_DIM), lambda b, g, i: (b, g, 0, 0)),
                  pl.BlockSpec((None, seq, MOBA_KV * HEAD_DIM), lambda b, g, i: (b, 0, 0)),
                  pl.BlockSpec((None, HEAD_DIM, seq), lambda b, g, i: (b, MOBA_KV + g, 0)),
                  pl.BlockSpec((None, 1, lanes), lambda b, g, i: (g, 0, 0))],
        out_specs=pl.BlockSpec((None, gw, MOBA_BLOCK), lambda b, g, i: (b, g, i)),
        out_shape=jax.ShapeDtypeStruct((bsz, MOBA_HEADS * HEAD_DIM, seq), BF16),
        scratch_shapes=[pltpu.VMEM((nblk, lanes), F32)],
        compiler_params=_cparams(("parallel", "parallel", "arbitrary")), name="moba_prompt",
    )(q_t, means, k, kv_t, slopes)


AB_COLS = dict(q_a=(0, 512), cmp=(512, 768), slc=(768, 1024), win=(1024, 1280), gate=(1280, 1304),
               q_b=(1304, 1816), kv_b=(1816, 2840))
AB_WIDTHS = (512, 256, 256, 256, 128, 512, 1024)


def _cols(w, name, lo=0, hi=None):
    a, b = AB_COLS[name]
    return w[:, a + lo:(b if hi is None else a + hi)]


def _ab_prompt_weights(w):
    w_std = jnp.concatenate([_cols(w, "cmp"), _cols(w, "slc", 0, 128), _cols(w, "win", 0, 128),
                             _cols(w, "kv_b", 0, 512)], axis=1).astype(BF16)
    wt = w.T
    g0 = AB_COLS["gate"][0]
    idx, keep = [], []
    for g in range(NSA_KV):
        for c in range(4):
            for r in range(NSA_GROUP):
                idx.append(g0 + (g * NSA_GROUP + r) * 3 + min(c, 2))
                keep.append(1.0 if c < 3 else 0.0)
    gate_rows = wt[jnp.asarray(idx)] * jnp.asarray(keep, F32)[:, None]
    rows = lambda name: wt[AB_COLS[name][0]:AB_COLS[name][1]]
    w_t = jnp.concatenate([rows("q_a"), rows("slc"), rows("win"), gate_rows, rows("q_b"), rows("kv_b")],
                          axis=0).astype(BF16)
    return w_std, w_t


AB_STD = ((0, 256, F32), (256, 128, BF16), (384, 128, BF16), (512, 512, BF16))
AB_TR = ((0, 512, F32), (512, 256, F32), (768, 256, F32), (1024, 32, F32), (1056, 512, BF16), (1568, 1024, F32))


def _group_slopes(n_heads, n_kv, tq):
    sl = jnp.asarray(_alibi_slopes(n_heads), F32).reshape(n_kv, n_heads // n_kv, 1)
    return jnp.broadcast_to(sl, (n_kv, n_heads // n_kv, tq)).reshape(n_kv, 1, (n_heads // n_kv) * tq)


def _cache_view(x_t, n_heads):
    bsz, _, seq = x_t.shape
    return jnp.transpose(x_t.reshape(bsz, 2, n_heads, HEAD_DIM, seq), (0, 4, 1, 2, 3))


def ab_prompt_mix(y, g_mix, w_std, w_t, c_pos, c_w1, c_w2, bsz, seq, tm):
    kv_cmp, ks, kw, kb, qa_t, kvs_t, kvw_t, gate_t, qb_t, kvb_t = rms_proj(
        y, g_mix, w_std, AB_STD, w_t, AB_TR, bsz, seq, tm)
    nb = seq // CMP_BLOCK

    x = jnp.transpose(kv_cmp.reshape(bsz, nb, CMP_BLOCK, 2, NSA_KV, HEAD_DIM), (3, 0, 1, 4, 2, 5))
    rows = bsz * nb * NSA_KV
    x = x.reshape(2, rows, CMP_BLOCK * HEAD_DIM)
    cmp = compress_blocks(x, c_pos.reshape(2, 1, CMP_BLOCK * HEAD_DIM), c_w1.astype(BF16), c_w2.astype(BF16),
                          min(rows, 256))
    cmp = cmp.reshape(2, bsz, nb, NSA_KV, HEAD_DIM)
    p_cmp = jnp.transpose(cmp, (1, 2, 0, 3, 4))
    kc = jnp.transpose(cmp[0], (0, 2, 1, 3))
    vct = jnp.transpose(cmp[1], (0, 2, 3, 1))

    kw_full = NSA_KV * HEAD_DIM
    oa_t = nsa_prompt(qa_t, gate_t, kc, vct, ks.reshape(bsz, seq, kw_full), kvs_t, kw.reshape(bsz, seq, kw_full),
                      kvw_t, _group_slopes(NSA_HEADS, NSA_KV, NSA_TQ))
    ob_t = sb_prompt(qb_t, kb.reshape(bsz, seq, SB_HEADS * HEAD_DIM), kvb_t)
    wlen = min(WINDOW, seq)
    return (jnp.concatenate([oa_t, ob_t], axis=1), _cache_view(kvs_t, NSA_KV), p_cmp,
            _cache_view(kvw_t[:, :, seq - wlen:], NSA_KV),
            _cache_view(kvb_t, SB_HEADS))


C_STD = ((0, 256, F32), (0, 256, BF16))
C_TR = ((0, 1024, F32), (1024, 512, F32))


def c_prompt_mix(y, g_mix, w_std, w_t, bsz, seq, tm):
    k32, k16, q_t, kv_t = rms_proj(y, g_mix, w_std, C_STD, w_t, C_TR, bsz, seq, tm)
    nblk = seq // MOBA_BLOCK
    means = moba_block_means(k32, bsz, seq).reshape(bsz, nblk, MOBA_KV, HEAD_DIM)
    means = jnp.transpose(means, (0, 2, 1, 3))
    o_t = moba_prompt(q_t, means, k16.reshape(bsz, seq, MOBA_KV * HEAD_DIM), kv_t,
                      _group_slopes(MOBA_HEADS, MOBA_KV, MOBA_BLOCK))
    return o_t, _cache_view(kv_t, MOBA_KV)


PAGES_PER_STEP = 8
N_SEL_OLD = SEL_TOPN - 1
BIG_IDX = 1 << 30


def _head_slopes(n_heads):
    return jnp.broadcast_to(jnp.asarray(_alibi_slopes(n_heads), F32).reshape(n_heads, 1), (n_heads, 128))


def _fold_heads(wide, n_groups, heads_per_group):
    n_heads = n_groups * heads_per_group
    hrow = _iota((n_heads, HEAD_DIM), 0) // heads_per_group
    out = jnp.zeros((n_heads, HEAD_DIM), F32)
    for g in range(n_groups):
        out = out + jnp.where(hrow == g, wide[:, g * HEAD_DIM:(g + 1) * HEAD_DIM], 0.0)
    return out


def _nsa_select_kernel(pt_ref, qm_ref, slope_ref, ccmp_ref, ocmp_ref, idx_ref, buf_ref, sem, *, n_pages):
    b = pl.program_id(0)
    row_copy = lambda p, page: pltpu.make_async_copy(ccmp_ref.at[pl.ds(page, 1), :], buf_ref.at[pl.ds(p, 1), :], sem)

    def issue(p, c):
        row_copy(p, pt_ref[b, p]).start()
        return c

    def drain(p, c):
        row_copy(p, 0).wait()
        return c

    lax.fori_loop(0, n_pages, issue, 0)
    lax.fori_loop(0, n_pages, drain, 0)

    buf = buf_ref[...]
    slope = slope_ref[:, 0:1]
    qpos = float(n_pages * PAGE_SIZE)
    page = _iota((1, n_pages), 1)
    st = _nt(qm_ref[...], buf, HI) * SCALE
    halves = []
    for blk in range(2):
        cend = (page * PAGE_SIZE + blk * CMP_BLOCK + (CMP_BLOCK - 1)).astype(F32)
        halves.append(st[blk * NSA_HEADS:(blk + 1) * NSA_HEADS] - slope * (qpos - cend))
    mx = jnp.maximum(jnp.max(halves[0], axis=1, keepdims=True), jnp.max(halves[1], axis=1, keepdims=True))
    e = [jnp.exp(hv - mx) for hv in halves]
    inv = 1.0 / (jnp.sum(e[0], axis=1, keepdims=True) + jnp.sum(e[1], axis=1, keepdims=True))
    p = [ev * inv for ev in e]
    v_lo = NSA_KV * HEAD_DIM
    ocmp_ref[...] = _dot(p[0], buf[:, v_lo:2 * v_lo], HI) + _dot(p[1], buf[:, 3 * v_lo:4 * v_lo], HI)

    hrow = _iota((NSA_HEADS, n_pages), 0) // NSA_GROUP
    jidx = (2 * _iota((2, n_pages), 1) + _iota((2, n_pages), 0)).astype(F32)
    lane = _iota((1, 128), 1)
    n_blk = 2 * n_pages
    rows = []
    for g in range(NSA_KV):
        imp = jnp.concatenate([jnp.sum(jnp.where(hrow == g, pv, 0.0), axis=0, keepdims=True) for pv in p], axis=0)
        cand = jnp.where((jidx == 0.0) | (jidx == float(n_blk - 1)), -1.0, imp)
        vec = jnp.where(lane == N_SEL_OLD - 1, float(n_blk - 1), 0.0)
        for r in range(N_SEL_OLD - 2):
            best = jnp.max(jnp.max(cand, axis=1, keepdims=True), axis=0, keepdims=True)
            hit = jnp.where(cand == best, jidx, float(BIG_IDX))
            pick = jnp.min(jnp.min(hit, axis=1, keepdims=True), axis=0, keepdims=True)
            vec = jnp.where(lane == r, pick, vec)
            cand = jnp.where(jidx == pick, -2.0, cand)
        rows.append(vec)
    idx_ref[...] = jnp.concatenate(rows + [jnp.zeros((8 - NSA_KV, 128), F32)], axis=0).astype(jnp.int32)


def nsa_sample_select(page_table, qm_cmp, c_cmp):
    bsz, n_pages = page_table.shape
    width = c_cmp.shape[1]
    assert 2 * n_pages - 2 >= N_SEL_OLD - 2
    return pl.pallas_call(
        functools.partial(_nsa_select_kernel, n_pages=n_pages),
        grid_spec=pltpu.PrefetchScalarGridSpec(
            num_scalar_prefetch=1,
            grid=(bsz,),
            in_specs=[pl.BlockSpec((None, 2 * NSA_HEADS, width), lambda b, pt: (b, 0, 0)),
                      pl.BlockSpec((NSA_HEADS, 128), lambda b, pt: (0, 0)),
                      pl.BlockSpec(memory_space=pl.ANY)],
            out_specs=[pl.BlockSpec((None, NSA_HEADS, 128), lambda b, pt: (b, 0, 0)),
                       pl.BlockSpec((None, 8, 128), lambda b, pt: (b, 0, 0))],
            scratch_shapes=[pltpu.VMEM((n_pages, width), F32), pltpu.SemaphoreType.DMA(())]),
        out_shape=[jax.ShapeDtypeStruct((bsz, NSA_HEADS, 128), F32),
                   jax.ShapeDtypeStruct((bsz, 8, 128), jnp.int32)],
        compiler_params=_cparams(("arbitrary",)), name="nsa_sample_select",
    )(page_table, qm_cmp, _head_slopes(NSA_HEADS), c_cmp)


def _nsa_attend_kernel(pt_ref, idx_ref, q_ref, slope_ref, gate_ref, ocmp_ref, new_slc_ref, new_win_ref,
                       new_win_t_ref, cwin_ref, cslc_ref, o_ref, swin_ref, kbuf_ref, vbuf_ref, sem, *, n_pages):
    b = pl.program_id(0)
    gw = NSA_KV * HEAD_DIM

    def page_copies(g, r):
        page = pt_ref[b, idx_ref[b, g * N_SEL_OLD + r] // 2]
        dst = pl.ds(r * PAGE_SIZE, PAGE_SIZE)
        return (pltpu.make_async_copy(cslc_ref.at[page, pl.ds(g * HEAD_DIM, HEAD_DIM), :],
                                      kbuf_ref.at[g, :, dst], sem),
                pltpu.make_async_copy(cslc_ref.at[page, pl.ds(gw + g * HEAD_DIM, HEAD_DIM), :],
                                      vbuf_ref.at[g, :, dst], sem))

    for g in range(NSA_KV):
        for r in range(N_SEL_OLD):
            for cp in page_copies(g, r):
                cp.start()
    for g in range(NSA_KV):
        for r in range(N_SEL_OLD):
            for cp in page_copies(g, r):
                cp.wait()

    q = q_ref[...]
    slope = slope_ref[:, 0:1]
    qpos = n_pages * PAGE_SIZE
    hgrp = _iota((NSA_HEADS, 1), 0) // NSA_GROUP
    n_keys = N_SEL_OLD * PAGE_SIZE
    lane = _iota((1, n_keys), 1)
    slot = lane // PAGE_SIZE
    tok = lane % PAGE_SIZE

    def per_group_rows(row):
        k = jnp.where(hgrp == 0, row[:, 0:HEAD_DIM], row[:, HEAD_DIM:2 * HEAD_DIM])
        v = jnp.where(hgrp == 0, row[:, gw:gw + HEAD_DIM], row[:, gw + HEAD_DIM:gw + 2 * HEAD_DIM])
        return k, v

    def finish(s, s_new, pv, v_new):
        mx = jnp.maximum(jnp.max(s, axis=1, keepdims=True), s_new)
        e = jnp.exp(s - mx)
        e_new = jnp.exp(s_new - mx)
        return (pv(e) + e_new * v_new) / (jnp.sum(e, axis=1, keepdims=True) + e_new)

    s_slc = jnp.zeros((NSA_HEADS, n_keys), F32)
    kpos = jnp.zeros((NSA_HEADS, n_keys), jnp.int32)
    valid = jnp.zeros((NSA_HEADS, n_keys), jnp.int32)
    for g in range(NSA_KV):
        s_slc = s_slc + _dot(jnp.where(hgrp == g, q, 0.0), kbuf_ref[g], HI)
        page_pos = jnp.zeros((1, n_keys), jnp.int32)
        half = jnp.zeros((1, n_keys), jnp.int32)
        for r in range(N_SEL_OLD):
            j = idx_ref[b, g * N_SEL_OLD + r]
            page_pos = jnp.where(slot == r, (j // 2) * PAGE_SIZE, page_pos)
            half = jnp.where(slot == r, j % 2, half)
        kpos = jnp.where(hgrp == g, page_pos + tok, kpos)
        valid = jnp.where(hgrp == g, jnp.where((tok // CMP_BLOCK) == half, 1, 0), valid)
    s_slc = jnp.where(valid > 0, s_slc * SCALE - slope * (qpos - kpos).astype(F32), NEG)
    k_new, v_new = per_group_rows(new_slc_ref[...])
    s_new = jnp.sum(q * k_new, axis=1, keepdims=True) * SCALE

    def pv_slc(e):
        return sum(_nt(jnp.where(hgrp == g, e, 0.0), vbuf_ref[g], HI) for g in range(NSA_KV))

    o_slc = finish(s_slc, s_new, pv_slc, v_new)

    cw = cwin_ref[...]
    wlen = cw.shape[1]
    wl = _iota((1, wlen), 1)
    q_groups = jnp.concatenate([jnp.where(hgrp == g, q, 0.0) for g in range(NSA_KV)], axis=1)
    s_w = _dot(q_groups, cw[0:gw], HI) * SCALE - slope * (wlen - wl).astype(F32)
    s_w = jnp.where(wl >= 1, s_w, NEG)
    k_new, v_new = per_group_rows(new_win_ref[...])
    s_new = jnp.sum(q * k_new, axis=1, keepdims=True) * SCALE
    o_win = finish(s_w, s_new, lambda e: _fold_heads(_nt(e, cw[gw:2 * gw], HI), NSA_KV, NSA_GROUP), v_new)

    new_t = new_win_t_ref[...]
    new_col = jnp.sum(jnp.where(_iota((1, new_t.shape[1]), 1) == b, new_t, 0.0), axis=1, keepdims=True)
    swin_ref[...] = jnp.where(wl == wlen - 1, new_col, pltpu.roll(cw, wlen - 1, 1))

    gate = 1.0 / (1.0 + jnp.exp(-gate_ref[...]))
    o_ref[...] = (gate[:, 0:1] * _fold_heads(ocmp_ref[...], NSA_KV, NSA_GROUP) + gate[:, 1:2] * o_slc
                  + gate[:, 2:3] * o_win)


def nsa_sample_attend(page_table, idx, q, gate, o_cmp, new_slc, new_win, new_win_t, c_win_t, c_slc_t):
    bsz, n_pages = page_table.shape
    width, wlen = c_win_t.shape[1:]
    per_b = lambda *shape: pl.BlockSpec((None,) + shape, lambda b, pt, ix: (b, 0, 0))
    n_keys = N_SEL_OLD * PAGE_SIZE
    return pl.pallas_call(
        functools.partial(_nsa_attend_kernel, n_pages=n_pages),
        grid_spec=pltpu.PrefetchScalarGridSpec(
            num_scalar_prefetch=2,
            grid=(bsz,),
            in_specs=[per_b(NSA_HEADS, HEAD_DIM),
                      pl.BlockSpec((NSA_HEADS, 128), lambda b, pt, ix: (0, 0)),
                      per_b(NSA_HEADS, 128), per_b(NSA_HEADS, 128), per_b(1, width), per_b(1, width),
                      pl.BlockSpec((None, width, bsz), lambda b, pt, ix: (0, 0, 0)),
                      per_b(width, wlen),
                      pl.BlockSpec(memory_space=pl.ANY)],
            out_specs=[per_b(NSA_HEADS, HEAD_DIM), per_b(width, wlen)],
            scratch_shapes=[pltpu.VMEM((NSA_KV, HEAD_DIM, n_keys), F32), pltpu.VMEM((NSA_KV, HEAD_DIM, n_keys), F32),
                            pltpu.SemaphoreType.DMA(())]),
        out_shape=[jax.ShapeDtypeStruct((bsz, NSA_HEADS, HEAD_DIM), F32),
                   jax.ShapeDtypeStruct((bsz, width, wlen), F32)],
        compiler_params=_cparams(("arbitrary",)), name="nsa_sample_attend",
    )(page_table, idx, q, _head_slopes(NSA_HEADS), gate, o_cmp, new_slc, new_win, new_win_t, c_win_t, c_slc_t)


def _sb_sample_kernel(pt_ref, q_ref, cache_ref, o_ref, buf_ref, sem, *, n_pages):
    b = pl.program_id(0)
    width = SB_HEADS * HEAD_DIM
    tk = PAGE_SIZE
    shape3 = (SB_HEADS, HEAD_DIM, tk)
    q3 = q_ref[...].reshape(shape3)
    lane = _iota((1, tk), 1)
    slot_of = lambda p: (n_pages - 1 - p) % 2
    page_copy = lambda p: pltpu.make_async_copy(cache_ref.at[pt_ref[b, p]], buf_ref.at[slot_of(p)],
                                                sem.at[slot_of(p)])
    page_copy(n_pages - 1).start()

    def more(state):
        p, _, run = state
        return (p >= 0) & (jnp.max(run) > SB_CUTOFF)

    def body(state):
        p, acc, run = state
        page_copy(p).wait()

        @pl.when(p >= 1)
        def _():
            page_copy(p - 1).start()

        page_ref = buf_ref.at[slot_of(p)]
        z = jnp.sum(page_ref[0:width, :].reshape(shape3) * q3, axis=1)
        sp = _softplus(z)
        ls = -sp
        suffix = ls
        for sh in (1, 2, 4, 8, 16, 32, 64):
            suffix = suffix + jnp.where(lane < tk - sh, pltpu.roll(suffix, tk - sh, 1), 0.0)
        w = jnp.exp(z - sp + (suffix - ls) + run)
        acc = acc + page_ref[width:2 * width, :].reshape(shape3) * w[:, None, :]
        return p - 1, acc, run + suffix[:, 0:1]

    p, acc, _ = lax.while_loop(more, body, (jnp.int32(n_pages - 1), jnp.zeros(shape3, F32),
                                            jnp.zeros((SB_HEADS, 1), F32)))

    @pl.when(p >= 0)
    def _():
        page_copy(p).wait()

    o_ref[...] = jnp.sum(acc.reshape(width, tk), axis=1, keepdims=True)


def sb_sample(page_table, q_lanes, c_sb_t):
    bsz, n_pages = page_table.shape
    width2 = c_sb_t.shape[1]
    return pl.pallas_call(
        functools.partial(_sb_sample_kernel, n_pages=n_pages),
        grid_spec=pltpu.PrefetchScalarGridSpec(
            num_scalar_prefetch=1,
            grid=(bsz,),
            in_specs=[pl.BlockSpec((None, width2 // 2, PAGE_SIZE), lambda b, pt: (b, 0, 0)),
                      pl.BlockSpec(memory_space=pl.ANY)],
            out_specs=pl.BlockSpec((None, width2 // 2, 1), lambda b, pt: (b, 0, 0)),
            scratch_shapes=[pltpu.VMEM((2, width2, PAGE_SIZE), F32), pltpu.SemaphoreType.DMA((2,))]),
        out_shape=jax.ShapeDtypeStruct((bsz, width2 // 2, 1), F32),
        compiler_params=_cparams(("arbitrary",)), name="sb_sample",
    )(page_table, q_lanes, c_sb_t)


def _moba_sample_kernel(pt_ref, qm_ref, slope_ref, new_ref, *refs, n_pages):
    page_refs = refs[:PAGES_PER_STEP]
    o_ref, obuf_ref, m_ref, l_ref, means_ref = refs[PAGES_PER_STEP:]
    s = pl.program_id(1)
    width = MOBA_KV * HEAD_DIM
    qm = qm_ref[...]
    qb = (qm * SCALE).astype(BF16)
    slope = slope_ref[:, 0:1]
    qpos = n_pages * PAGE_SIZE
    lane = _iota((1, PAGE_SIZE), 1)
    plane = _iota((1, n_pages), 1)
    pages_per_block = MOBA_BLOCK // PAGE_SIZE

    @pl.when(s == 0)
    def _():
        m_ref[...] = jnp.zeros_like(m_ref)
        l_ref[...] = jnp.zeros_like(l_ref)
        means_ref[...] = jnp.zeros_like(means_ref)

    m_all = m_ref[...]
    l_all = l_ref[...]
    means = means_ref[...]
    keys = []
    for u, page_ref in enumerate(page_refs):
        pg = s * PAGES_PER_STEP + u
        kf = page_ref[0:width, :]
        sc = _dot(qb, kf.astype(BF16)) - slope * (qpos - (pg * PAGE_SIZE + lane)).astype(F32)
        mx = jnp.max(sc, axis=1, keepdims=True)
        e = jnp.exp(sc - mx)
        obuf_ref[pg] = _nt(e.astype(BF16), page_ref[width:2 * width, :].astype(BF16))
        m_all = jnp.where(plane == pg, mx, m_all)
        l_all = jnp.where(plane == pg, jnp.sum(e, axis=1, keepdims=True), l_all)
        keys.append(kf)
        if len(keys) == pages_per_block:
            mean = jnp.sum(sum(keys[1:], keys[0]), axis=1, keepdims=True) * (1.0 / MOBA_BLOCK)
            means = jnp.where(plane // pages_per_block == pg // pages_per_block, mean, means)
            keys = []
    m_ref[...] = m_all
    l_ref[...] = l_all
    means_ref[...] = means

    @pl.when(s == pl.num_programs(1) - 1)
    def _():
        gsc = _dot(qm, means, HI)
        cand = gsc
        blk_of_page = (plane // pages_per_block).astype(F32)
        sel = jnp.zeros(gsc.shape, jnp.bool_)
        for _ in range(MOBA_TOPK):
            best = jnp.max(cand, axis=1, keepdims=True)
            first = jnp.min(jnp.where(cand == best, blk_of_page, float(BIG_IDX)), axis=1, keepdims=True)
            hit = blk_of_page == first
            sel = sel | hit
            cand = jnp.where(hit, NEG, cand)
        new = new_ref[...]
        s_new = jnp.sum(qm * new[:, 0:width], axis=1, keepdims=True) * SCALE
        top = jnp.maximum(jnp.max(jnp.where(sel, m_all, NEG), axis=1, keepdims=True), s_new)
        coef = jnp.where(sel, jnp.exp(m_all - top), 0.0)
        e_new = jnp.exp(s_new - top)
        den = jnp.sum(coef * l_all, axis=1, keepdims=True) + e_new
        wide = e_new * new[:, width:2 * width]
        for p in range(n_pages):
            wide = wide + coef[:, p:p + 1] * obuf_ref[p]
        o_ref[...] = _fold_heads(wide, MOBA_KV, MOBA_GROUP) / den


def moba_sample(page_table, qm, new_kv, c_moba_t):
    bsz, n_pages = page_table.shape
    width2 = c_moba_t.shape[1]
    assert n_pages % PAGES_PER_STEP == 0 and n_pages * PAGE_SIZE >= MOBA_TOPK * MOBA_BLOCK

    def page_spec(u):
        return pl.BlockSpec((None, width2, PAGE_SIZE), lambda b, s, pt: (pt[b, s * PAGES_PER_STEP + u], 0, 0))

    return pl.pallas_call(
        functools.partial(_moba_sample_kernel, n_pages=n_pages),
        grid_spec=pltpu.PrefetchScalarGridSpec(
            num_scalar_prefetch=1,
            grid=(bsz, n_pages // PAGES_PER_STEP),
            in_specs=[pl.BlockSpec((None, MOBA_HEADS, width2 // 2), lambda b, s, pt: (b, 0, 0)),
                      pl.BlockSpec((MOBA_HEADS, 128), lambda b, s, pt: (0, 0)),
                      pl.BlockSpec((None, 1, width2), lambda b, s, pt: (b, 0, 0))]
                     + [page_spec(u) for u in range(PAGES_PER_STEP)],
            out_specs=pl.BlockSpec((None, MOBA_HEADS, HEAD_DIM), lambda b, s, pt: (b, 0, 0)),
            scratch_shapes=[pltpu.VMEM((n_pages, MOBA_HEADS, width2 // 2), F32),
                            pltpu.VMEM((MOBA_HEADS, n_pages), F32),
                            pltpu.VMEM((MOBA_HEADS, n_pages), F32),
                            pltpu.VMEM((width2 // 2, n_pages), F32)]),
        out_shape=jax.ShapeDtypeStruct((bsz, MOBA_HEADS, HEAD_DIM), F32),
        compiler_params=_cparams(("parallel", "arbitrary")), name="moba_sample",
    )(page_table, qm, _head_slopes(MOBA_HEADS), new_kv, *([c_moba_t] * PAGES_PER_STEP))


def _block_diag_query(q, n_groups):
    bsz, n_heads, d = q.shape
    onehot = (jnp.arange(n_heads)[:, None] // (n_heads // n_groups) == jnp.arange(n_groups)[None, :]).astype(q.dtype)
    return (q[:, :, None, :] * onehot[None, :, :, None]).reshape(bsz, n_heads, n_groups * d)


def _pages_feature_major(cache):
    n_phys, page = cache.shape[:2]
    return jnp.transpose(cache, (0, 2, 3, 4, 1)).reshape(n_phys, -1, page)


def _ab_sample_weights(w):
    g0, g1 = AB_COLS["gate"]
    w_std = jnp.concatenate([w[:, :g0], jnp.pad(w[:, g0:g1], ((0, 0), (0, 128 - N_GATE))), w[:, g1:]],
                            axis=1).astype(BF16)
    w_t = w.T[AB_COLS["win"][0]:AB_COLS["win"][1]].astype(BF16)
    return w_std, w_t


def ab_sample_mix(y, g_mix, w_std, w_t, c_slc, c_cmp, c_win, c_sb, page_table):
    bsz = y.shape[0]
    n_phys = c_slc.shape[0]
    std = []
    off = 0
    for wd in AB_WIDTHS:
        std.append((off, wd, F32))
        off += wd
    width = 2 * NSA_KV * HEAD_DIM
    q_a, kv_cmp, kv_slc, kv_win, gate, q_b, kv_b, kvw_t = rms_proj(y, g_mix, w_std, std, w_t, ((0, width, F32),),
                                                                   1, bsz, bsz)
    q_a = q_a.reshape(bsz, NSA_HEADS, HEAD_DIM)
    qm_kv = jnp.pad(_block_diag_query(q_a, NSA_KV), ((0, 0), (0, 0), (0, NSA_KV * HEAD_DIM)))
    zeros = jnp.zeros_like(qm_kv)
    qm_cmp = jnp.concatenate([jnp.concatenate([qm_kv, zeros], -1), jnp.concatenate([zeros, qm_kv], -1)], axis=1)
    o_cmp, idx = nsa_sample_select(page_table, qm_cmp, c_cmp.reshape(n_phys, -1))
    gate3 = jnp.pad(gate[:, :N_GATE].reshape(bsz, NSA_HEADS, 3), ((0, 0), (0, 0), (0, 125)))
    c_win_t = jnp.transpose(c_win, (0, 2, 3, 4, 1)).reshape(bsz, width, -1)
    o_a, s_win_t = nsa_sample_attend(page_table, idx[:, :NSA_KV, :N_SEL_OLD].reshape(bsz, -1), q_a, gate3, o_cmp,
                                     kv_slc.reshape(bsz, 1, width), kv_win.reshape(bsz, 1, width), kvw_t,
                                     c_win_t, _pages_feature_major(c_slc))
    q_lanes = jnp.broadcast_to((q_b * SCALE)[:, :, None], (bsz, SB_HEADS * HEAD_DIM, PAGE_SIZE))
    o_b = sb_sample(page_table, q_lanes, _pages_feature_major(c_sb))
    o_ts = jnp.concatenate([o_a.reshape(bsz, -1), o_b.reshape(bsz, -1)], axis=1).T[None].astype(BF16)
    shape5 = (bsz, 1, 2, NSA_KV, HEAD_DIM)
    s_win = jnp.transpose(s_win_t.reshape(bsz, 2, NSA_KV, HEAD_DIM, -1), (0, 4, 1, 2, 3))
    return (o_ts, kv_slc.reshape(shape5), kv_cmp.reshape(shape5), s_win,
            kv_b.reshape(bsz, 1, 2, SB_HEADS, HEAD_DIM))


def c_sample_mix(y, g_mix, w_std, c_moba, page_table):
    bsz = y.shape[0]
    q, kv = rms_proj(y, g_mix, w_std, ((0, 1024, F32), (1024, 512, F32)), None, (), 1, bsz, bsz)
    qm = _block_diag_query(q.reshape(bsz, MOBA_HEADS, HEAD_DIM), MOBA_KV)
    o = moba_sample(page_table, qm, kv.reshape(bsz, 1, -1), _pages_feature_major(c_moba))
    return o.reshape(bsz, -1).T[None].astype(BF16), kv.reshape(bsz, 1, 2, MOBA_KV, HEAD_DIM)


def kernel(x_prompt, x_sample, cache_nsa_slc, cache_nsa_cmp, cache_nsa_win, cache_sb, cache_moba, page_table,
           norm_mix, w_in_ab, w_out_ab, cmp_pos, cmp_w1, cmp_w2, w_in_c, w_out_c,
           norm_ffn, w_up, w_down, norm_final):
    bsz, seq, d = x_prompt.shape
    tm = 512
    w_up_b = w_up.astype(BF16)
    w_down_b = w_down.astype(BF16)
    dbsz = x_sample.shape[0]
    yp = x_prompt.reshape(bsz * seq, d)
    ys = x_sample.reshape(dbsz, d)
    w_oab = w_out_ab[0].astype(BF16)
    w_oc = w_out_c[0].astype(BF16)
    wc = w_in_c[0]
    k_lo, k_hi = MOBA_HEADS * HEAD_DIM, (MOBA_HEADS + MOBA_KV) * HEAD_DIM

    ab_std, ab_t = _ab_prompt_weights(w_in_ab[0])
    o_ts, p_slc, p_cmp, p_win, p_sb = ab_prompt_mix(yp, norm_mix[0], ab_std, ab_t, cmp_pos[0], cmp_w1[0], cmp_w2[0],
                                                    bsz, seq, tm)
    yp = mix_mlp(yp, o_ts, w_oab, norm_ffn[0], w_up_b[0], w_down_b[0], norm_final, False, bsz, seq, tm)
    o_ts, p_moba = c_prompt_mix(yp, norm_mix[1], wc[:, k_lo:k_hi].astype(BF16), wc.T.astype(BF16), bsz, seq, tm)
    yp = mix_mlp(yp, o_ts, w_oc, norm_ffn[1], w_up_b[1], w_down_b[1], norm_final, True, bsz, seq, tm)

    abs_std, abs_t = _ab_sample_weights(w_in_ab[0])
    o_ts, s_slc, s_cmp, s_win, s_sb = ab_sample_mix(ys, norm_mix[0], abs_std, abs_t, cache_nsa_slc[0],
                                                    cache_nsa_cmp[0], cache_nsa_win[0], cache_sb[0], page_table)
    ys = mix_mlp(ys, o_ts, w_oab, norm_ffn[0], w_up_b[0], w_down_b[0], norm_final, False, 1, dbsz, dbsz)
    o_ts, s_moba = c_sample_mix(ys, norm_mix[1], wc.astype(BF16), cache_moba[0], page_table)
    ys = mix_mlp(ys, o_ts, w_oc, norm_ffn[1], w_up_b[1], w_down_b[1], norm_final, True, 1, dbsz, dbsz)

    return (yp.reshape(bsz, seq, d), ys.reshape(dbsz, 1, d), p_slc[None], p_cmp[None], p_win[None], p_sb[None],
            p_moba[None], s_slc[None], s_cmp[None], s_win[None], s_sb[None], s_moba[None])
```

```python
import functools

import jax
import jax.numpy as jnp
from jax import lax
from jax.experimental import pallas as pl
from jax.experimental.pallas import tpu as pltpu

HEAD_DIM = 64
NSA_HEADS = 8
NSA_KV = 2
NSA_GROUP = NSA_HEADS // NSA_KV
CMP_BLOCK = 64
SEL_TOPN = 16
WINDOW = 512
SB_HEADS = 8
MOBA_HEADS = 16
MOBA_KV = 4
MOBA_GROUP = MOBA_HEADS // MOBA_KV
MOBA_BLOCK = 256
MOBA_TOPK = 3
PAGE_SIZE = 128
EPS = 1e-6
NEG = -1e30
MASK_BIAS = -(2.0 ** 100)
FORCE = 1e4
SCALE = HEAD_DIM ** -0.5

F32 = jnp.float32
BF16 = jnp.bfloat16
HI = lax.Precision.HIGHEST

VMEM_LIMIT = 48 * 1024 * 1024

NSA_TQ = 128
NSA_TK = 256
SB_T = 256
SB_CUTOFF = -110.0
EXP_CUTOFF = -110.0
NORM_SLACK = 1.02
AUG = 16
N_GATE = 3 * NSA_HEADS


def _cparams(sem):
    return pltpu.CompilerParams(dimension_semantics=sem, vmem_limit_bytes=VMEM_LIMIT)


def _alibi_slopes(n):
    return [2.0 ** (-8.0 * (h + 1) / n) for h in range(n)]


def _nt(a, b, precision=None):
    return lax.dot_general(a, b, (((1,), (1,)), ((), ())), precision=precision, preferred_element_type=F32)


def _tn(a, b):
    return lax.dot_general(a, b, (((0,), (0,)), ((), ())), preferred_element_type=F32)


def _dot(a, b, precision=None):
    return jnp.dot(a, b, precision=precision, preferred_element_type=F32)


def _rms(x, g):
    return x * lax.rsqrt(jnp.mean(x * x, axis=-1, keepdims=True) + EPS) * g


def _softplus(z):
    return jnp.maximum(z, 0.0) + jnp.log(1.0 + jnp.exp(-jnp.abs(z)))


def _iota(shape, axis):
    return lax.broadcasted_iota(jnp.int32, shape, axis)


def _proj_kernel(*refs, std, tr):
    x_ref, g_ref = refs[:2]
    refs = refs[2:]
    ws_ref = wt_ref = None
    if std:
        ws_ref, refs = refs[0], refs[1:]
    if tr:
        wt_ref, refs = refs[0], refs[1:]
    h = _rms(x_ref[...], g_ref[...]).astype(BF16)
    k = 0
    for off, wd, _ in std:
        refs[k][...] = _dot(h, ws_ref[:, off:off + wd]).astype(refs[k].dtype)
        k += 1
    for off, rows, _ in tr:
        refs[k][...] = _nt(wt_ref[off:off + rows, :], h).astype(refs[k].dtype)
        k += 1


def rms_proj(x, g, w_std, std, w_t, tr, bsz, seq, tm):
    m, d = x.shape
    nj = seq // tm
    assert m == bsz * seq and seq % tm == 0
    const = lambda i: (0, 0)
    operands = [x, g.reshape(1, d)]
    in_specs = [pl.BlockSpec((tm, d), lambda i: (i, 0)), pl.BlockSpec((1, d), const)]
    if std:
        operands.append(w_std)
        in_specs.append(pl.BlockSpec(w_std.shape, const))
    if tr:
        operands.append(w_t)
        in_specs.append(pl.BlockSpec(w_t.shape, const))
    out_specs = ([pl.BlockSpec((tm, wd), lambda i: (i, 0)) for _, wd, _ in std]
                 + [pl.BlockSpec((None, rows, tm), lambda i: (i // nj, 0, i % nj)) for _, rows, _ in tr])
    out_shape = ([jax.ShapeDtypeStruct((m, wd), dt) for _, wd, dt in std]
                 + [jax.ShapeDtypeStruct((bsz, rows, seq), dt) for _, rows, dt in tr])
    return pl.pallas_call(
        functools.partial(_proj_kernel, std=tuple(std), tr=tuple(tr)),
        grid=(m // tm,), in_specs=in_specs, out_specs=out_specs, out_shape=out_shape,
        compiler_params=_cparams(("parallel",)), name="rms_proj",
    )(*operands)


def _mix_mlp_kernel(y_ref, o_ref, wo_ref, g_ref, wu_ref, wd_ref, gf_ref, out_ref, *, ff_chunk, final):
    y = y_ref[...] + _tn(o_ref[...], wo_ref[...])
    h = _rms(y, g_ref[...]).astype(BF16)
    acc = y
    for c in range(0, wu_ref.shape[1], ff_chunk):
        a = jnp.maximum(_dot(h, wu_ref[:, c:c + ff_chunk]), 0.0)
        acc = acc + _dot((a * a).astype(BF16), wd_ref[c:c + ff_chunk, :])
    if final:
        acc = _rms(acc, gf_ref[...])
    out_ref[...] = acc


def mix_mlp(y, o_t, w_out, g_ffn, w_up, w_down, g_final, final, bsz, seq, tm):
    m, d = y.shape
    f = w_up.shape[1]
    nj = seq // tm
    const = lambda i: (0, 0)
    return pl.pallas_call(
        functools.partial(_mix_mlp_kernel, ff_chunk=min(f, 1024), final=final),
        grid=(m // tm,),
        in_specs=[pl.BlockSpec((tm, d), lambda i: (i, 0)),
                  pl.BlockSpec((None, o_t.shape[1], tm), lambda i: (i // nj, 0, i % nj)),
                  pl.BlockSpec((d, d), const), pl.BlockSpec((1, d), const), pl.BlockSpec((d, f), const),
                  pl.BlockSpec((f, d), const), pl.BlockSpec((1, d), const)],
        out_specs=pl.BlockSpec((tm, d), lambda i: (i, 0)),
        out_shape=jax.ShapeDtypeStruct((m, d), F32),
        compiler_params=_cparams(("parallel",)), name="mix_mlp",
    )(y, o_t, w_out, g_ffn.reshape(1, d), w_up, w_down, g_final.reshape(1, d))


def _compress_kernel(x_ref, pos_ref, w1_ref, w2_ref, o_ref):
    x = (x_ref[...] + pos_ref[...]).astype(BF16)
    hid = _dot(x, w1_ref[...])
    hid = hid * (1.0 / (1.0 + jnp.exp(-hid)))
    o_ref[...] = _dot(hid.astype(BF16), w2_ref[...])


def compress_blocks(x, pos, w1_bf16, w2_bf16, tr):
    _, r, fdim = x.shape
    hdim = w1_bf16.shape[2]
    return pl.pallas_call(
        _compress_kernel,
        grid=(2, r // tr),
        in_specs=[pl.BlockSpec((None, tr, fdim), lambda k, i: (k, i, 0)),
                  pl.BlockSpec((None, 1, fdim), lambda k, i: (k, 0, 0)),
                  pl.BlockSpec((None, fdim, hdim), lambda k, i: (k, 0, 0)),
                  pl.BlockSpec((None, hdim, HEAD_DIM), lambda k, i: (k, 0, 0))],
        out_specs=pl.BlockSpec((None, tr, HEAD_DIM), lambda k, i: (k, i, 0)),
        out_shape=jax.ShapeDtypeStruct((2, r, HEAD_DIM), F32),
        compiler_params=_cparams(("parallel", "parallel")), name="nsa_compress",
    )(x, pos, w1_bf16, w2_bf16)


def _biased_update(carry, s, shift, vt):
    m, l, acc = carry
    m_new = jnp.maximum(m, jnp.max(s, axis=0, keepdims=True) + shift)
    alpha = jnp.exp(m - m_new)
    p = jnp.exp(s - (m_new - shift))
    l = alpha * l + jnp.sum(p, axis=0, keepdims=True)
    acc = alpha * acc + _dot(vt, p.astype(BF16))
    return m_new, l, acc


def _slope_rows(slope, lanes):
    p1 = slope.astype(BF16).astype(F32)
    p2 = (slope - p1).astype(BF16).astype(F32)
    p3 = slope - p1 - p2
    return jnp.concatenate([p1, p2, p3, jnp.zeros((5, lanes), F32)], axis=0)


def _softmax_init(lanes):
    return (jnp.full((1, lanes), NEG, F32), jnp.zeros((1, lanes), F32), jnp.zeros((HEAD_DIM, lanes), F32))


def _join(chains):
    return tuple(jnp.concatenate(parts, axis=1) for parts in zip(*chains))


def _split(joined, n):
    w = joined[0].shape[1] // n
    return [tuple(x[:, r * w:(r + 1) * w] for x in joined) for r in range(n)]


def _nsa_prompt_kernel(qt_ref, gate_ref, kc_ref, vct_ref, ks_ref, vst_ref, kw_ref, vwt_ref, slope_ref,
                       o_ref, selneg_ref, *, nb):
    g = pl.program_id(1)
    i = pl.program_id(2)
    tq, tk = NSA_TQ, NSA_TK
    lanes = NSA_GROUP * tq
    blocks_per_chunk = tk // CMP_BLOCK
    qt = jnp.concatenate([qt_ref[r * HEAD_DIM:(r + 1) * HEAD_DIM, :] for r in range(NSA_GROUP)], axis=1)
    qs = qt * SCALE
    rowgrp = _iota((NSA_KV * HEAD_DIM, 1), 0) // HEAD_DIM
    qpad = jnp.where(rowgrp == g, jnp.concatenate([qs] * NSA_KV, axis=0), 0.0).astype(BF16)
    slope = slope_ref[...]
    t_row = i * tq + _iota((1, lanes), 1) % tq
    tf = t_row.astype(F32)

    cend = _iota((nb, 1), 0) * CMP_BLOCK + (CMP_BLOCK - 1)
    sc = _dot(kc_ref[...], qs, HI) - slope * (tf - cend.astype(F32))
    vis = cend <= t_row
    sc = jnp.where(vis, sc, NEG)
    e = jnp.where(vis, jnp.exp(sc - jnp.max(sc, axis=0, keepdims=True)), 0.0)
    den = jnp.sum(e, axis=0, keepdims=True)
    p = e * jnp.where(den > 0.0, 1.0 / den, 0.0)
    o_cmp = _dot(vct_ref[...], p, HI)

    imp = p[:, 0:tq]
    for r in range(1, NSA_GROUP):
        imp = imp + p[:, r * tq:(r + 1) * tq]
    j = _iota((nb, tq), 0)
    own = (i * tq + _iota((nb, tq), 1)) // CMP_BLOCK
    forced = (j == 0) | (j == own) | (j == own - 1)
    score = jnp.where(forced, FORCE, imp)
    score = jnp.where(j <= own, score, -1.0)
    rank = jnp.zeros((nb, tq), F32)
    for m in range(nb):
        sm = score[m:m + 1, :]
        beats = (sm > score) | ((sm == score) & (j > m))
        rank = rank + jnp.where(beats, 1.0, 0.0)
    selneg = jnp.where((rank < float(min(SEL_TOPN, nb))) & (score >= 0.0), 0.0, MASK_BIAS)
    for c in range(nb // blocks_per_chunk):
        selneg_ref[c] = jnp.concatenate([selneg[c * blocks_per_chunk:(c + 1) * blocks_per_chunk],
                                         jnp.zeros((8 - blocks_per_chunk, tq), F32)], axis=0)

    sub = _iota((tk, 1), 0)
    col = _iota((tk, AUG), 1)
    aug = jnp.where(col < blocks_per_chunk, jnp.where(sub // CMP_BLOCK == col, 1.0, 0.0),
                    jnp.where((col >= 8) & (col < 11), sub.astype(F32), 0.0)).astype(BF16)
    slope8 = _slope_rows(slope, lanes)
    cd = (i * tq) // tk
    rel = (i * tq - cd * tk) + _iota((1, lanes), 1) % tq
    zeros8 = jnp.zeros((8, lanes), F32)
    causal = jnp.where(sub <= rel, 0.0, MASK_BIAS)
    win_tail = jnp.where((sub > rel) & (cd >= 2), 0.0, MASK_BIAS)

    def scores(k_ref, c, mask_rows):
        ks = pl.multiple_of(c * tk, tk)
        bias_rows = jnp.concatenate([mask_rows, slope8], axis=0).astype(BF16)
        return _dot(k_ref[pl.ds(ks, tk), :], qpad) + _dot(aug, bias_rows)

    def update(carry, s, vt_ref, c):
        ks = pl.multiple_of(c * tk, tk)
        return _biased_update(carry, s, slope * ((c * tk).astype(F32) - tf), vt_ref[:, pl.ds(ks, tk)].astype(BF16))

    def sel_rows(c):
        return jnp.concatenate([selneg_ref[c]] * NSA_GROUP, axis=1)

    init = _softmax_init(lanes)
    carry = update(init, scores(ks_ref, cd, sel_rows(cd)) + causal, vst_ref, cd)

    def slc_body(c, state):
        carry, s = state
        nxt = jnp.minimum(c + 1, jnp.maximum(cd - 1, 0))
        s_next = scores(ks_ref, nxt, sel_rows(nxt))
        return update(carry, s, vst_ref, c), s_next

    (_, l_s, acc_s), _ = lax.fori_loop(0, cd, slc_body, (carry, scores(ks_ref, 0, sel_rows(0))))

    c1, c2 = jnp.maximum(cd - 1, 0), jnp.maximum(cd - 2, 0)
    s0 = scores(kw_ref, cd, zeros8) + causal
    s1 = scores(kw_ref, c1, jnp.where(cd >= 1, zeros8, MASK_BIAS))
    s2 = scores(kw_ref, c2, zeros8) + win_tail
    _, l_w, acc_w = update(update(update(init, s0, vwt_ref, cd), s1, vwt_ref, c1), s2, vwt_ref, c2)

    gates = []
    for c in range(3):
        row = jnp.concatenate([gate_ref[c * NSA_GROUP + r:c * NSA_GROUP + r + 1, :] for r in range(NSA_GROUP)],
                              axis=1)
        gates.append(1.0 / (1.0 + jnp.exp(-row)))
    o = gates[0] * o_cmp + gates[1] * (acc_s / l_s) + gates[2] * (acc_w / l_w)
    for r in range(NSA_GROUP):
        o_ref[r * HEAD_DIM:(r + 1) * HEAD_DIM, :] = o[:, r * tq:(r + 1) * tq].astype(o_ref.dtype)


def nsa_prompt(qa_t, gate_t, kc, vct, ks, kvs_t, kw, kvw_t, slopes):
    bsz, _, seq = qa_t.shape
    nb = kc.shape[2]
    tq = NSA_TQ
    lanes = NSA_GROUP * tq
    gw = NSA_GROUP * HEAD_DIM
    assert seq % NSA_TK == 0 and WINDOW == 2 * NSA_TK and NSA_TK % NSA_TQ == 0
    return pl.pallas_call(
        functools.partial(_nsa_prompt_kernel, nb=nb),
        grid=(bsz, NSA_KV, seq // tq),
        in_specs=[pl.BlockSpec((None, gw, tq), lambda b, g, i: (b, g, i)),
                  pl.BlockSpec((None, 16, tq), lambda b, g, i: (b, g, i)),
                  pl.BlockSpec((None, None, nb, HEAD_DIM), lambda b, g, i: (b, g, 0, 0)),
                  pl.BlockSpec((None, None, HEAD_DIM, nb), lambda b, g, i: (b, g, 0, 0)),
                  pl.BlockSpec((None, seq, NSA_KV * HEAD_DIM), lambda b, g, i: (b, 0, 0)),
                  pl.BlockSpec((None, HEAD_DIM, seq), lambda b, g, i: (b, NSA_KV + g, 0)),
                  pl.BlockSpec((None, seq, NSA_KV * HEAD_DIM), lambda b, g, i: (b, 0, 0)),
                  pl.BlockSpec((None, HEAD_DIM, seq), lambda b, g, i: (b, NSA_KV + g, 0)),
                  pl.BlockSpec((None, 1, lanes), lambda b, g, i: (g, 0, 0))],
        out_specs=pl.BlockSpec((None, gw, tq), lambda b, g, i: (b, g, i)),
        out_shape=jax.ShapeDtypeStruct((bsz, NSA_HEADS * HEAD_DIM, seq), BF16),
        scratch_shapes=[pltpu.VMEM((nb * CMP_BLOCK // NSA_TK, 8, tq), F32)],
        compiler_params=_cparams(("parallel", "parallel", "arbitrary")), name="nsa_prompt",
    )(qa_t, gate_t, kc, vct, ks, kvs_t, kw, kvw_t, slopes)


def _sb_prompt_kernel(q_ref, k_ref, vt_ref, o_ref):
    i = pl.program_id(2)
    t = SB_T
    qpair = (q_ref[...].astype(F32) * SCALE).astype(BF16)
    rowh = _iota((2 * HEAD_DIM, 1), 0) // HEAD_DIM
    qcat = jnp.concatenate([jnp.where(rowh == hh, qpair, jnp.zeros_like(qpair)) for hh in range(2)], axis=1)
    th = t // 2
    after = jnp.where(_iota((th, th), 0) < _iota((th, th), 1), 1.0, 0.0).astype(BF16)
    vis = _iota((t, 1), 0) < _iota((1, 2 * t), 1) % t

    def logits(kb):
        return _dot(k_ref[pl.ds(pl.multiple_of(kb * t, t), t), :], qcat)

    def step(kb, carry, z, diagonal):
        acc, run = carry
        ks = pl.multiple_of(kb * t, t)
        sp = _softplus(z)
        ls = -sp
        if diagonal:
            ls = jnp.where(vis, ls, 0.0)
        hi = ls.astype(BF16)
        lo = (ls - hi.astype(F32)).astype(BF16)
        later = []
        for half in (1, 0):
            rows = slice(half * th, (half + 1) * th)
            later.insert(0, _dot(after, hi[rows]) + _dot(after, lo[rows]) + run)
            run = run + jnp.sum(ls[rows], axis=0, keepdims=True)
        w = jnp.exp(z - sp + jnp.concatenate(later, axis=0))
        if diagonal:
            w = jnp.where(vis, w, 0.0)
        w = w.astype(BF16)
        pv = [_dot(vt_ref[hh * HEAD_DIM:(hh + 1) * HEAD_DIM, pl.ds(ks, t)].astype(BF16), w[:, hh * t:(hh + 1) * t])
              for hh in range(2)]
        return acc + jnp.concatenate(pv, axis=1), run

    def body(state):
        st, carry, z = state
        kb = i - st
        z_next = logits(jnp.maximum(kb - 1, 0))
        return st + 1, step(kb, carry, z, False), z_next

    def more(state):
        st, (_, run), _ = state
        return (st <= i) & (jnp.max(run) > SB_CUTOFF)

    carry = step(i, (jnp.zeros((HEAD_DIM, 2 * t), F32), jnp.zeros((1, 2 * t), F32)), logits(i), True)
    _, (acc, _), _ = lax.while_loop(more, body, (jnp.int32(1), carry, logits(jnp.maximum(i - 1, 0))))
    for hh in range(2):
        o_ref[hh * HEAD_DIM:(hh + 1) * HEAD_DIM, :] = acc[:, hh * t:(hh + 1) * t].astype(o_ref.dtype)


def sb_prompt(qb_t, kb, kvb_t):
    bsz, width, seq = qb_t.shape
    pw = 2 * HEAD_DIM
    n_pairs = width // pw
    return pl.pallas_call(
        _sb_prompt_kernel,
        grid=(bsz, n_pairs, seq // SB_T),
        in_specs=[pl.BlockSpec((None, pw, SB_T), lambda b, p, i: (b, p, i)),
                  pl.BlockSpec((None, seq, pw), lambda b, p, i: (b, 0, p)),
                  pl.BlockSpec((None, pw, seq), lambda b, p, i: (b, n_pairs + p, 0))],
        out_specs=pl.BlockSpec((None, pw, SB_T), lambda b, p, i: (b, p, i)),
        out_shape=jax.ShapeDtypeStruct((bsz, width, seq), BF16),
        compiler_params=_cparams(("parallel", "parallel", "arbitrary")), name="sb_prompt",
    )(qb_t, kb, kvb_t)


def _block_mean_kernel(k_ref, o_ref, n2_ref):
    k = k_ref[...]
    o_ref[...] = jnp.sum(k, axis=0, keepdims=True) * (1.0 / MOBA_BLOCK)
    sq = k * k
    rows = [jnp.broadcast_to(jnp.max(jnp.sum(sq[:, g * HEAD_DIM:(g + 1) * HEAD_DIM], axis=1, keepdims=True),
                                     axis=0, keepdims=True), (1, HEAD_DIM)) for g in range(MOBA_KV)]
    n2_ref[...] = jnp.concatenate(rows, axis=1)


def moba_block_means(k, bsz, seq):
    nblk = seq // MOBA_BLOCK
    width = k.shape[1]
    out = pl.BlockSpec((None, None, 1, width), lambda b, n: (b, n, 0, 0))
    return pl.pallas_call(
        _block_mean_kernel,
        grid=(bsz, nblk),
        in_specs=[pl.BlockSpec((MOBA_BLOCK, width), lambda b, n: (b * nblk + n, 0))],
        out_specs=[out, out],
        out_shape=[jax.ShapeDtypeStruct((bsz, nblk, 1, width), F32)] * 2,
        compiler_params=_cparams(("parallel", "parallel")), name="moba_block_means",
    )(k)


def _moba_prompt_kernel(qt_ref, means_ref, kmax_ref, k_ref, vt_ref, slope_ref, o_ref, selneg_ref, *, nblk):
    g = pl.program_id(1)
    i = pl.program_id(2)
    tq = MOBA_BLOCK
    lanes = MOBA_GROUP * tq
    qt = jnp.concatenate([qt_ref[r * HEAD_DIM:(r + 1) * HEAD_DIM, :] for r in range(MOBA_GROUP)], axis=1)
    qs = qt * SCALE
    rowgrp = _iota((MOBA_KV * HEAD_DIM, 1), 0) // HEAD_DIM
    qpad = jnp.where(rowgrp == g, jnp.concatenate([qs] * MOBA_KV, axis=0), 0.0).astype(BF16)
    slope = slope_ref[...]
    t_row = i * tq + _iota((1, lanes), 1) % tq
    tf = t_row.astype(F32)

    gsc = _dot(means_ref[...], qt, HI)
    n_idx = _iota((nblk, lanes), 0)
    past = n_idx < i
    gsc = jnp.where(past, gsc, NEG)
    rank = jnp.zeros((nblk, lanes), F32)
    for m in range(nblk):
        gm = gsc[m:m + 1, :]
        beats = (gm > gsc) | ((gm == gsc) & (n_idx > m))
        rank = rank + jnp.where(beats, 1.0, 0.0)
    selneg_ref[...] = jnp.where((rank < float(min(MOBA_TOPK, nblk))) & past, 0.0, MASK_BIAS)

    sub = _iota((tq, 1), 0)
    ramp = slope * sub.astype(F32)

    def scores(n):
        return _dot(k_ref[pl.ds(pl.multiple_of(n * tq, tq), tq), :], qpad) + ramp

    def update(carry, s, n, mask_row):
        shift = slope * ((n * tq).astype(F32) - tf) + mask_row
        return _biased_update(carry, s, shift, vt_ref[:, pl.ds(pl.multiple_of(n * tq, tq), tq)].astype(BF16))

    causal = jnp.where(sub <= _iota((1, lanes), 1) % tq, 0.0, MASK_BIAS)
    carry = update(_softmax_init(lanes), scores(i) + causal, i, 0.0)

    bound = jnp.sqrt(jnp.sum(qs * qs, axis=0, keepdims=True)) * kmax_ref[:, 0:1] * NORM_SLACK
    reach = (bound - carry[0] - EXP_CUTOFF) / slope
    need = jnp.clip(jnp.ceil((reach - 1.0) * (1.0 / tq)), 0.0, float(nblk))
    n_lo = jnp.maximum(i - jnp.max(need, axis=1, keepdims=True).astype(jnp.int32)[0, 0], 0)

    def body(n, state):
        carry, s = state
        s_next = scores(jnp.minimum(n + 1, jnp.maximum(i - 1, 0)))
        return update(carry, s, n, selneg_ref[pl.ds(n, 1), :]), s_next

    (_, l, acc), _ = lax.fori_loop(n_lo, i, body, (carry, scores(jnp.minimum(n_lo, jnp.maximum(i - 1, 0)))))
    o = acc / l
    for r in range(MOBA_GROUP):
        o_ref[r * HEAD_DIM:(r + 1) * HEAD_DIM, :] = o[:, r * tq:(r + 1) * tq].astype(o_ref.dtype)


def moba_prompt(q_t, means, kmax, k, kv_t, slopes):
    bsz, _, seq = q_t.shape
    nblk = seq // MOBA_BLOCK
    gw = MOBA_GROUP * HEAD_DIM
    lanes = MOBA_GROUP * MOBA_BLOCK
    return pl.pallas_call(
        functools.partial(_moba_prompt_kernel, nblk=nblk),
        grid=(bsz, MOBA_KV, nblk),
        in_specs=[pl.BlockSpec((None, gw, MOBA_BLOCK), lambda b, g, i: (b, g, i)),
                  pl.BlockSpec((None, None, nblk, HEAD_DIM), lambda b, g, i: (b, g, 0, 0)),
                  pl.BlockSpec((None, None, 1, 128), lambda b, g, i: (b, g, 0, 0)),
                  pl.BlockSpec((None, seq, MOBA_KV * HEAD_DIM), lambda b, g, i: (b, 0, 0)),
                  pl.BlockSpec((None, HEAD_DIM, seq), lambda b, g, i: (b, MOBA_KV + g, 0)),
                  pl.BlockSpec((None, 1, lanes), lambda b, g, i: (g, 0, 0))],
        out_specs=pl.BlockSpec((None, gw, MOBA_BLOCK), lambda b, g, i: (b, g, i)),
        out_shape=jax.ShapeDtypeStruct((bsz, MOBA_HEADS * HEAD_DIM, seq), BF16),
        scratch_shapes=[pltpu.VMEM((nblk, lanes), F32)],
        compiler_params=_cparams(("parallel", "parallel", "arbitrary")), name="moba_prompt",
    )(q_t, means, kmax, k, kv_t, slopes)


AB_COLS = dict(q_a=(0, 512), cmp=(512, 768), slc=(768, 1024), win=(1024, 1280), gate=(1280, 1304),
               q_b=(1304, 1816), kv_b=(1816, 2840))
AB_WIDTHS = (512, 256, 256, 256, 128, 512, 1024)


def _cols(w, name, lo=0, hi=None):
    a, b = AB_COLS[name]
    return w[:, a + lo:(b if hi is None else a + hi)]


def _ab_prompt_weights(w):
    w_std = jnp.concatenate([_cols(w, "cmp"), _cols(w, "slc", 0, 128), _cols(w, "win", 0, 128),
                             _cols(w, "kv_b", 0, 512)], axis=1).astype(BF16)
    wt = w.T
    g0 = AB_COLS["gate"][0]
    idx, keep = [], []
    for g in range(NSA_KV):
        for c in range(4):
            for r in range(NSA_GROUP):
                idx.append(g0 + (g * NSA_GROUP + r) * 3 + min(c, 2))
                keep.append(1.0 if c < 3 else 0.0)
    gate_rows = wt[jnp.asarray(idx)] * jnp.asarray(keep, F32)[:, None]
    rows = lambda name: wt[AB_COLS[name][0]:AB_COLS[name][1]]
    w_t = jnp.concatenate([rows("q_a"), rows("slc"), rows("win"), gate_rows, rows("q_b"), rows("kv_b")],
                          axis=0).astype(BF16)
    return w_std, w_t


AB_STD = ((0, 256, F32), (256, 128, BF16), (384, 128, BF16), (512, 512, BF16))
AB_TR = ((0, 512, F32), (512, 256, F32), (768, 256, F32), (1024, 32, F32), (1056, 512, BF16), (1568, 1024, F32))


def _group_slopes(n_heads, n_kv, tq):
    sl = jnp.asarray(_alibi_slopes(n_heads), F32).reshape(n_kv, n_heads // n_kv, 1)
    return jnp.broadcast_to(sl, (n_kv, n_heads // n_kv, tq)).reshape(n_kv, 1, (n_heads // n_kv) * tq)


def _cache_view(x_t, n_heads):
    bsz, _, seq = x_t.shape
    return jnp.transpose(x_t.reshape(bsz, 2, n_heads, HEAD_DIM, seq), (0, 4, 1, 2, 3))


def ab_prompt_mix(y, g_mix, w_std, w_t, c_pos, c_w1, c_w2, bsz, seq, tm):
    kv_cmp, ks, kw, kb, qa_t, kvs_t, kvw_t, gate_t, qb_t, kvb_t = rms_proj(
        y, g_mix, w_std, AB_STD, w_t, AB_TR, bsz, seq, tm)
    nb = seq // CMP_BLOCK

    x = jnp.transpose(kv_cmp.reshape(bsz, nb, CMP_BLOCK, 2, NSA_KV, HEAD_DIM), (3, 0, 1, 4, 2, 5))
    rows = bsz * nb * NSA_KV
    x = x.reshape(2, rows, CMP_BLOCK * HEAD_DIM)
    cmp = compress_blocks(x, c_pos.reshape(2, 1, CMP_BLOCK * HEAD_DIM), c_w1.astype(BF16), c_w2.astype(BF16),
                          min(rows, 256))
    cmp = cmp.reshape(2, bsz, nb, NSA_KV, HEAD_DIM)
    p_cmp = jnp.transpose(cmp, (1, 2, 0, 3, 4))
    kc = jnp.transpose(cmp[0], (0, 2, 1, 3))
    vct = jnp.transpose(cmp[1], (0, 2, 3, 1))

    kw_full = NSA_KV * HEAD_DIM
    oa_t = nsa_prompt(qa_t, gate_t, kc, vct, ks.reshape(bsz, seq, kw_full), kvs_t, kw.reshape(bsz, seq, kw_full),
                      kvw_t, _group_slopes(NSA_HEADS, NSA_KV, NSA_TQ))
    ob_t = sb_prompt(qb_t, kb.reshape(bsz, seq, SB_HEADS * HEAD_DIM), kvb_t)
    wlen = min(WINDOW, seq)
    return (jnp.concatenate([oa_t, ob_t], axis=1), _cache_view(kvs_t, NSA_KV), p_cmp,
            _cache_view(kvw_t[:, :, seq - wlen:], NSA_KV),
            _cache_view(kvb_t, SB_HEADS))


C_STD = ((0, 256, F32), (0, 256, BF16))
C_TR = ((0, 1024, F32), (1024, 512, F32))


def c_prompt_mix(y, g_mix, w_std, w_t, bsz, seq, tm):
    k32, k16, q_t, kv_t = rms_proj(y, g_mix, w_std, C_STD, w_t, C_TR, bsz, seq, tm)
    nblk = seq // MOBA_BLOCK
    means, norm2 = moba_block_means(k32, bsz, seq)
    means = jnp.transpose(means.reshape(bsz, nblk, MOBA_KV, HEAD_DIM), (0, 2, 1, 3))
    kmax = jnp.sqrt(jnp.max(norm2.reshape(bsz, nblk, MOBA_KV, HEAD_DIM)[..., 0], axis=1))
    kmax = jnp.broadcast_to(kmax[:, :, None, None], (bsz, MOBA_KV, 1, 128))
    o_t = moba_prompt(q_t, means, kmax, k16.reshape(bsz, seq, MOBA_KV * HEAD_DIM), kv_t,
                      _group_slopes(MOBA_HEADS, MOBA_KV, MOBA_BLOCK))
    return o_t, _cache_view(kv_t, MOBA_KV)


PAGES_PER_STEP = 8
N_SEL_OLD = SEL_TOPN - 1
BIG_IDX = 1 << 30


def _head_slopes(n_heads):
    return jnp.broadcast_to(jnp.asarray(_alibi_slopes(n_heads), F32).reshape(n_heads, 1), (n_heads, 128))


def _fold_heads(wide, n_groups, heads_per_group):
    n_heads = n_groups * heads_per_group
    hrow = _iota((n_heads, HEAD_DIM), 0) // heads_per_group
    out = jnp.zeros((n_heads, HEAD_DIM), F32)
    for g in range(n_groups):
        out = out + jnp.where(hrow == g, wide[:, g * HEAD_DIM:(g + 1) * HEAD_DIM], 0.0)
    return out


def _nsa_select_kernel(pt_ref, qm_ref, slope_ref, ccmp_ref, ocmp_ref, idx_ref, buf_ref, sem, *, n_pages):
    b = pl.program_id(0)
    row_copy = lambda p, page: pltpu.make_async_copy(ccmp_ref.at[pl.ds(page, 1), :], buf_ref.at[pl.ds(p, 1), :], sem)

    def issue(p, c):
        row_copy(p, pt_ref[b, p]).start()
        return c

    def drain(p, c):
        row_copy(p, 0).wait()
        return c

    lax.fori_loop(0, n_pages, issue, 0)
    lax.fori_loop(0, n_pages, drain, 0)

    buf = buf_ref[...]
    slope = slope_ref[:, 0:1]
    qpos = float(n_pages * PAGE_SIZE)
    page = _iota((1, n_pages), 1)
    st = _nt(qm_ref[...], buf, HI) * SCALE
    halves = []
    for blk in range(2):
        cend = (page * PAGE_SIZE + blk * CMP_BLOCK + (CMP_BLOCK - 1)).astype(F32)
        halves.append(st[blk * NSA_HEADS:(blk + 1) * NSA_HEADS] - slope * (qpos - cend))
    mx = jnp.maximum(jnp.max(halves[0], axis=1, keepdims=True), jnp.max(halves[1], axis=1, keepdims=True))
    e = [jnp.exp(hv - mx) for hv in halves]
    inv = 1.0 / (jnp.sum(e[0], axis=1, keepdims=True) + jnp.sum(e[1], axis=1, keepdims=True))
    p = [ev * inv for ev in e]
    v_lo = NSA_KV * HEAD_DIM
    ocmp_ref[...] = _dot(p[0], buf[:, v_lo:2 * v_lo], HI) + _dot(p[1], buf[:, 3 * v_lo:4 * v_lo], HI)

    hrow = _iota((NSA_HEADS, n_pages), 0) // NSA_GROUP
    jidx = (2 * _iota((2, n_pages), 1) + _iota((2, n_pages), 0)).astype(F32)
    lane = _iota((1, 128), 1)
    n_blk = 2 * n_pages
    rows = []
    for g in range(NSA_KV):
        imp = jnp.concatenate([jnp.sum(jnp.where(hrow == g, pv, 0.0), axis=0, keepdims=True) for pv in p], axis=0)
        cand = jnp.where((jidx == 0.0) | (jidx == float(n_blk - 1)), -1.0, imp)
        vec = jnp.where(lane == N_SEL_OLD - 1, float(n_blk - 1), 0.0)
        for r in range(N_SEL_OLD - 2):
            best = jnp.max(jnp.max(cand, axis=1, keepdims=True), axis=0, keepdims=True)
            hit = jnp.where(cand == best, jidx, float(BIG_IDX))
            pick = jnp.min(jnp.min(hit, axis=1, keepdims=True), axis=0, keepdims=True)
            vec = jnp.where(lane == r, pick, vec)
            cand = jnp.where(jidx == pick, -2.0, cand)
        rows.append(vec)
    idx_ref[...] = jnp.concatenate(rows + [jnp.zeros((8 - NSA_KV, 128), F32)], axis=0).astype(jnp.int32)


def nsa_sample_select(page_table, qm_cmp, c_cmp):
    bsz, n_pages = page_table.shape
    width = c_cmp.shape[1]
    assert 2 * n_pages - 2 >= N_SEL_OLD - 2
    return pl.pallas_call(
        functools.partial(_nsa_select_kernel, n_pages=n_pages),
        grid_spec=pltpu.PrefetchScalarGridSpec(
            num_scalar_prefetch=1,
            grid=(bsz,),
            in_specs=[pl.BlockSpec((None, 2 * NSA_HEADS, width), lambda b, pt: (b, 0, 0)),
                      pl.BlockSpec((NSA_HEADS, 128), lambda b, pt: (0, 0)),
                      pl.BlockSpec(memory_space=pl.ANY)],
            out_specs=[pl.BlockSpec((None, NSA_HEADS, 128), lambda b, pt: (b, 0, 0)),
                       pl.BlockSpec((None, 8, 128), lambda b, pt: (b, 0, 0))],
            scratch_shapes=[pltpu.VMEM((n_pages, width), F32), pltpu.SemaphoreType.DMA(())]),
        out_shape=[jax.ShapeDtypeStruct((bsz, NSA_HEADS, 128), F32),
                   jax.ShapeDtypeStruct((bsz, 8, 128), jnp.int32)],
        compiler_params=_cparams(("arbitrary",)), name="nsa_sample_select",
    )(page_table, qm_cmp, _head_slopes(NSA_HEADS), c_cmp)


def _nsa_attend_kernel(pt_ref, idx_ref, q_ref, slope_ref, gate_ref, ocmp_ref, new_slc_ref, new_win_ref,
                       new_win_t_ref, cwin_ref, cslc_ref, o_ref, swin_ref, kbuf_ref, vbuf_ref, sem, *, n_pages):
    b = pl.program_id(0)
    gw = NSA_KV * HEAD_DIM

    def page_copies(g, r):
        page = pt_ref[b, idx_ref[b, g * N_SEL_OLD + r] // 2]
        dst = pl.ds(r * PAGE_SIZE, PAGE_SIZE)
        return (pltpu.make_async_copy(cslc_ref.at[page, pl.ds(g * HEAD_DIM, HEAD_DIM), :],
                                      kbuf_ref.at[g, :, dst], sem),
                pltpu.make_async_copy(cslc_ref.at[page, pl.ds(gw + g * HEAD_DIM, HEAD_DIM), :],
                                      vbuf_ref.at[g, :, dst], sem))

    for g in range(NSA_KV):
        for r in range(N_SEL_OLD):
            for cp in page_copies(g, r):
                cp.start()
    for g in range(NSA_KV):
        for r in range(N_SEL_OLD):
            for cp in page_copies(g, r):
                cp.wait()

    q = q_ref[...]
    slope = slope_ref[:, 0:1]
    qpos = n_pages * PAGE_SIZE
    hgrp = _iota((NSA_HEADS, 1), 0) // NSA_GROUP
    n_keys = N_SEL_OLD * PAGE_SIZE
    lane = _iota((1, n_keys), 1)
    slot = lane // PAGE_SIZE
    tok = lane % PAGE_SIZE

    def per_group_rows(row):
        k = jnp.where(hgrp == 0, row[:, 0:HEAD_DIM], row[:, HEAD_DIM:2 * HEAD_DIM])
        v = jnp.where(hgrp == 0, row[:, gw:gw + HEAD_DIM], row[:, gw + HEAD_DIM:gw + 2 * HEAD_DIM])
        return k, v

    def finish(s, s_new, pv, v_new):
        mx = jnp.maximum(jnp.max(s, axis=1, keepdims=True), s_new)
        e = jnp.exp(s - mx)
        e_new = jnp.exp(s_new - mx)
        return (pv(e) + e_new * v_new) / (jnp.sum(e, axis=1, keepdims=True) + e_new)

    s_slc = jnp.zeros((NSA_HEADS, n_keys), F32)
    kpos = jnp.zeros((NSA_HEADS, n_keys), jnp.int32)
    valid = jnp.zeros((NSA_HEADS, n_keys), jnp.int32)
    for g in range(NSA_KV):
        s_slc = s_slc + _dot(jnp.where(hgrp == g, q, 0.0), kbuf_ref[g], HI)
        page_pos = jnp.zeros((1, n_keys), jnp.int32)
        half = jnp.zeros((1, n_keys), jnp.int32)
        for r in range(N_SEL_OLD):
            j = idx_ref[b, g * N_SEL_OLD + r]
            page_pos = jnp.where(slot == r, (j // 2) * PAGE_SIZE, page_pos)
            half = jnp.where(slot == r, j % 2, half)
        kpos = jnp.where(hgrp == g, page_pos + tok, kpos)
        valid = jnp.where(hgrp == g, jnp.where((tok // CMP_BLOCK) == half, 1, 0), valid)
    s_slc = jnp.where(valid > 0, s_slc * SCALE - slope * (qpos - kpos).astype(F32), NEG)
    k_new, v_new = per_group_rows(new_slc_ref[...])
    s_new = jnp.sum(q * k_new, axis=1, keepdims=True) * SCALE

    def pv_slc(e):
        return sum(_nt(jnp.where(hgrp == g, e, 0.0), vbuf_ref[g], HI) for g in range(NSA_KV))

    o_slc = finish(s_slc, s_new, pv_slc, v_new)

    cw = cwin_ref[...]
    wlen = cw.shape[1]
    wl = _iota((1, wlen), 1)
    q_groups = jnp.concatenate([jnp.where(hgrp == g, q, 0.0) for g in range(NSA_KV)], axis=1)
    s_w = _dot(q_groups, cw[0:gw], HI) * SCALE - slope * (wlen - wl).astype(F32)
    s_w = jnp.where(wl >= 1, s_w, NEG)
    k_new, v_new = per_group_rows(new_win_ref[...])
    s_new = jnp.sum(q * k_new, axis=1, keepdims=True) * SCALE
    o_win = finish(s_w, s_new, lambda e: _fold_heads(_nt(e, cw[gw:2 * gw], HI), NSA_KV, NSA_GROUP), v_new)

    new_t = new_win_t_ref[...]
    new_col = jnp.sum(jnp.where(_iota((1, new_t.shape[1]), 1) == b, new_t, 0.0), axis=1, keepdims=True)
    swin_ref[...] = jnp.where(wl == wlen - 1, new_col, pltpu.roll(cw, wlen - 1, 1))

    gate = 1.0 / (1.0 + jnp.exp(-gate_ref[...]))
    o_ref[...] = (gate[:, 0:1] * _fold_heads(ocmp_ref[...], NSA_KV, NSA_GROUP) + gate[:, 1:2] * o_slc
                  + gate[:, 2:3] * o_win)


def nsa_sample_attend(page_table, idx, q, gate, o_cmp, new_slc, new_win, new_win_t, c_win_t, c_slc_t):
    bsz, n_pages = page_table.shape
    width, wlen = c_win_t.shape[1:]
    per_b = lambda *shape: pl.BlockSpec((None,) + shape, lambda b, pt, ix: (b, 0, 0))
    n_keys = N_SEL_OLD * PAGE_SIZE
    return pl.pallas_call(
        functools.partial(_nsa_attend_kernel, n_pages=n_pages),
        grid_spec=pltpu.PrefetchScalarGridSpec(
            num_scalar_prefetch=2,
            grid=(bsz,),
            in_specs=[per_b(NSA_HEADS, HEAD_DIM),
                      pl.BlockSpec((NSA_HEADS, 128), lambda b, pt, ix: (0, 0)),
                      per_b(NSA_HEADS, 128), per_b(NSA_HEADS, 128), per_b(1, width), per_b(1, width),
                      pl.BlockSpec((None, width, bsz), lambda b, pt, ix: (0, 0, 0)),
                      per_b(width, wlen),
                      pl.BlockSpec(memory_space=pl.ANY)],
            out_specs=[per_b(NSA_HEADS, HEAD_DIM), per_b(width, wlen)],
            scratch_shapes=[pltpu.VMEM((NSA_KV, HEAD_DIM, n_keys), F32), pltpu.VMEM((NSA_KV, HEAD_DIM, n_keys), F32),
                            pltpu.SemaphoreType.DMA(())]),
        out_shape=[jax.ShapeDtypeStruct((bsz, NSA_HEADS, HEAD_DIM), F32),
                   jax.ShapeDtypeStruct((bsz, width, wlen), F32)],
        compiler_params=_cparams(("arbitrary",)), name="nsa_sample_attend",
    )(page_table, idx, q, _head_slopes(NSA_HEADS), gate, o_cmp, new_slc, new_win, new_win_t, c_win_t, c_slc_t)


def _sb_sample_kernel(pt_ref, q_ref, cache_ref, o_ref, buf_ref, sem, *, n_pages):
    b = pl.program_id(0)
    width = SB_HEADS * HEAD_DIM
    tk = PAGE_SIZE
    shape3 = (SB_HEADS, HEAD_DIM, tk)
    q3 = q_ref[...].reshape(shape3)
    lane = _iota((1, tk), 1)
    slot_of = lambda p: (n_pages - 1 - p) % 2
    page_copy = lambda p: pltpu.make_async_copy(cache_ref.at[pt_ref[b, p]], buf_ref.at[slot_of(p)],
                                                sem.at[slot_of(p)])
    page_copy(n_pages - 1).start()

    def more(state):
        p, _, run = state
        return (p >= 0) & (jnp.max(run) > SB_CUTOFF)

    def body(state):
        p, acc, run = state
        page_copy(p).wait()

        @pl.when(p >= 1)
        def _():
            page_copy(p - 1).start()

        page_ref = buf_ref.at[slot_of(p)]
        z = jnp.sum(page_ref[0:width, :].reshape(shape3) * q3, axis=1)
        sp = _softplus(z)
        ls = -sp
        suffix = ls
        for sh in (1, 2, 4, 8, 16, 32, 64):
            suffix = suffix + jnp.where(lane < tk - sh, pltpu.roll(suffix, tk - sh, 1), 0.0)
        w = jnp.exp(z - sp + (suffix - ls) + run)
        acc = acc + page_ref[width:2 * width, :].reshape(shape3) * w[:, None, :]
        return p - 1, acc, run + suffix[:, 0:1]

    p, acc, _ = lax.while_loop(more, body, (jnp.int32(n_pages - 1), jnp.zeros(shape3, F32),
                                            jnp.zeros((SB_HEADS, 1), F32)))

    @pl.when(p >= 0)
    def _():
        page_copy(p).wait()

    o_ref[...] = jnp.sum(acc.reshape(width, tk), axis=1, keepdims=True)


def sb_sample(page_table, q_lanes, c_sb_t):
    bsz, n_pages = page_table.shape
    width2 = c_sb_t.shape[1]
    return pl.pallas_call(
        functools.partial(_sb_sample_kernel, n_pages=n_pages),
        grid_spec=pltpu.PrefetchScalarGridSpec(
            num_scalar_prefetch=1,
            grid=(bsz,),
            in_specs=[pl.BlockSpec((None, width2 // 2, PAGE_SIZE), lambda b, pt: (b, 0, 0)),
                      pl.BlockSpec(memory_space=pl.ANY)],
            out_specs=pl.BlockSpec((None, width2 // 2, 1), lambda b, pt: (b, 0, 0)),
            scratch_shapes=[pltpu.VMEM((2, width2, PAGE_SIZE), F32), pltpu.SemaphoreType.DMA((2,))]),
        out_shape=jax.ShapeDtypeStruct((bsz, width2 // 2, 1), F32),
        compiler_params=_cparams(("arbitrary",)), name="sb_sample",
    )(page_table, q_lanes, c_sb_t)


def _moba_select_kernel(pt_ref, qm_ref, *refs, n_pages):
    page_refs = refs[:PAGES_PER_STEP]
    idx_ref, means_ref = refs[PAGES_PER_STEP:]
    s = pl.program_id(1)
    pages_per_block = MOBA_BLOCK // PAGE_SIZE
    n_blk = n_pages // pages_per_block
    blane = _iota((1, 128), 1)

    @pl.when(s == 0)
    def _():
        means_ref[...] = jnp.zeros_like(means_ref)

    means = means_ref[...]
    keys = []
    for u, page_ref in enumerate(page_refs):
        keys.append(page_ref[...])
        if len(keys) == pages_per_block:
            mean = jnp.sum(sum(keys[1:], keys[0]), axis=1, keepdims=True) * (1.0 / MOBA_BLOCK)
            means = jnp.where(blane == (s * PAGES_PER_STEP + u) // pages_per_block, mean, means)
            keys = []
    means_ref[...] = means

    @pl.when(s == pl.num_programs(1) - 1)
    def _():
        gsc = jnp.where(blane < n_blk, _dot(qm_ref[...], means, HI), NEG)
        blk = blane.astype(F32)
        vec = jnp.zeros((MOBA_HEADS, 128), F32)
        for r in range(MOBA_TOPK):
            best = jnp.max(gsc, axis=1, keepdims=True)
            first = jnp.min(jnp.where(gsc == best, blk, float(BIG_IDX)), axis=1, keepdims=True)
            vec = jnp.where(blane == r, first, vec)
            gsc = jnp.where(blk == first, NEG, gsc)
        idx_ref[...] = vec.astype(jnp.int32)


def moba_sample_select(page_table, qm, c_moba_t):
    bsz, n_pages = page_table.shape
    width = c_moba_t.shape[1] // 2
    assert n_pages % PAGES_PER_STEP == 0 and MOBA_TOPK <= n_pages * PAGE_SIZE // MOBA_BLOCK <= 128

    def page_spec(u):
        return pl.BlockSpec((None, width, PAGE_SIZE), lambda b, s, pt: (pt[b, s * PAGES_PER_STEP + u], 0, 0))

    return pl.pallas_call(
        functools.partial(_moba_select_kernel, n_pages=n_pages),
        grid_spec=pltpu.PrefetchScalarGridSpec(
            num_scalar_prefetch=1,
            grid=(bsz, n_pages // PAGES_PER_STEP),
            in_specs=[pl.BlockSpec((None, MOBA_HEADS, width), lambda b, s, pt: (b, 0, 0))]
                     + [page_spec(u) for u in range(PAGES_PER_STEP)],
            out_specs=pl.BlockSpec((None, MOBA_HEADS, 128), lambda b, s, pt: (b, 0, 0)),
            scratch_shapes=[pltpu.VMEM((width, 128), F32)]),
        out_shape=jax.ShapeDtypeStruct((bsz, MOBA_HEADS, 128), jnp.int32),
        compiler_params=_cparams(("parallel", "arbitrary")), name="moba_sample_select",
    )(page_table, qm, *([c_moba_t] * PAGES_PER_STEP))


def _moba_attend_kernel(pt_ref, idx_ref, q_ref, knew_ref, vnew_ref, slope_ref, cache_ref, o_ref, kbuf_ref, vbuf_ref,
                        sem, *, n_pages):
    b = pl.program_id(0)
    pages_per_block = MOBA_BLOCK // PAGE_SIZE
    n_sel = MOBA_TOPK * pages_per_block
    width = MOBA_KV * HEAD_DIM

    def page_copies(h, u):
        g = h // MOBA_GROUP
        page = pt_ref[b, idx_ref[b, h * MOBA_TOPK + u // pages_per_block] * pages_per_block + u % pages_per_block]
        dst = pl.ds(u * PAGE_SIZE, PAGE_SIZE)
        return (pltpu.make_async_copy(cache_ref.at[page, pl.ds(g * HEAD_DIM, HEAD_DIM), :],
                                      kbuf_ref.at[h, :, dst], sem),
                pltpu.make_async_copy(cache_ref.at[page, pl.ds(width + g * HEAD_DIM, HEAD_DIM), :],
                                      vbuf_ref.at[h, :, dst], sem))

    for h in range(MOBA_HEADS):
        for u in range(n_sel):
            for cp in page_copies(h, u):
                cp.start()
    for h in range(MOBA_HEADS):
        for u in range(n_sel):
            for cp in page_copies(h, u):
                cp.wait()

    n_keys = n_sel * PAGE_SIZE
    shape3 = (MOBA_HEADS, HEAD_DIM, PAGE_SIZE)
    q3 = q_ref[...].reshape(shape3)
    z = jnp.sum(kbuf_ref[...] * jnp.concatenate([q3] * n_sel, axis=2), axis=1)
    lane = _iota((1, n_keys), 1)
    hrow = _iota((MOBA_HEADS, 1), 0)
    kpos = jnp.zeros((MOBA_HEADS, n_keys), jnp.int32)
    for h in range(MOBA_HEADS):
        row = lane % MOBA_BLOCK
        for r in range(MOBA_TOPK):
            row = row + jnp.where(lane // MOBA_BLOCK == r, idx_ref[b, h * MOBA_TOPK + r] * MOBA_BLOCK, 0)
        kpos = jnp.where(hrow == h, row, kpos)
    qpos = n_pages * PAGE_SIZE
    sc = z - slope_ref[:, 0:1] * (qpos - kpos).astype(F32)
    knew = knew_ref[...].reshape(MOBA_HEADS, HEAD_DIM, 1)
    s_new = jnp.sum(q3[:, :, 0:1] * knew, axis=1)
    mx = jnp.maximum(jnp.max(sc, axis=1, keepdims=True), s_new)
    e = jnp.exp(sc - mx)
    e_new = jnp.exp(s_new - mx)
    den = jnp.sum(e, axis=1, keepdims=True) + e_new
    pv = vbuf_ref[...] * e[:, None, :]
    tiles = pv[:, :, 0:PAGE_SIZE]
    for u in range(1, n_sel):
        tiles = tiles + pv[:, :, u * PAGE_SIZE:(u + 1) * PAGE_SIZE]
    o = jnp.sum(tiles, axis=2, keepdims=True) + e_new[:, :, None] * vnew_ref[...].reshape(MOBA_HEADS, HEAD_DIM, 1)
    o_ref[...] = (o / den[:, :, None]).reshape(MOBA_HEADS * HEAD_DIM, 1)


def moba_sample_attend(page_table, idx, q_lanes, knew, vnew, c_moba_t):
    bsz, n_pages = page_table.shape
    hd = MOBA_HEADS * HEAD_DIM
    n_keys = MOBA_TOPK * MOBA_BLOCK
    per_b = lambda *shape: pl.BlockSpec((None,) + shape, lambda b, pt, ix: (b, 0, 0))
    return pl.pallas_call(
        functools.partial(_moba_attend_kernel, n_pages=n_pages),
        grid_spec=pltpu.PrefetchScalarGridSpec(
            num_scalar_prefetch=2,
            grid=(bsz,),
            in_specs=[per_b(hd, PAGE_SIZE), per_b(hd, 1), per_b(hd, 1),
                      pl.BlockSpec((MOBA_HEADS, 128), lambda b, pt, ix: (0, 0)),
                      pl.BlockSpec(memory_space=pl.ANY)],
            out_specs=per_b(hd, 1),
            scratch_shapes=[pltpu.VMEM((MOBA_HEADS, HEAD_DIM, n_keys), F32),
                            pltpu.VMEM((MOBA_HEADS, HEAD_DIM, n_keys), F32),
                            pltpu.SemaphoreType.DMA(())]),
        out_shape=jax.ShapeDtypeStruct((bsz, hd, 1), F32),
        compiler_params=_cparams(("arbitrary",)), name="moba_sample_attend",
    )(page_table, idx, q_lanes, knew, vnew, _head_slopes(MOBA_HEADS), c_moba_t)


def _block_diag_query(q, n_groups):
    bsz, n_heads, d = q.shape
    onehot = (jnp.arange(n_heads)[:, None] // (n_heads // n_groups) == jnp.arange(n_groups)[None, :]).astype(q.dtype)
    return (q[:, :, None, :] * onehot[None, :, :, None]).reshape(bsz, n_heads, n_groups * d)


def _pages_feature_major(cache):
    n_phys, page = cache.shape[:2]
    return jnp.transpose(cache, (0, 2, 3, 4, 1)).reshape(n_phys, -1, page)


def _ab_sample_weights(w):
    g0, g1 = AB_COLS["gate"]
    w_std = jnp.concatenate([w[:, :g0], jnp.pad(w[:, g0:g1], ((0, 0), (0, 128 - N_GATE))), w[:, g1:]],
                            axis=1).astype(BF16)
    w_t = w.T[AB_COLS["win"][0]:AB_COLS["win"][1]].astype(BF16)
    return w_std, w_t


def ab_sample_mix(y, g_mix, w_std, w_t, c_slc, c_cmp, c_win, c_sb, page_table):
    bsz = y.shape[0]
    n_phys = c_slc.shape[0]
    std = []
    off = 0
    for wd in AB_WIDTHS:
        std.append((off, wd, F32))
        off += wd
    width = 2 * NSA_KV * HEAD_DIM
    q_a, kv_cmp, kv_slc, kv_win, gate, q_b, kv_b, kvw_t = rms_proj(y, g_mix, w_std, std, w_t, ((0, width, F32),),
                                                                   1, bsz, bsz)
    q_a = q_a.reshape(bsz, NSA_HEADS, HEAD_DIM)
    qm_kv = jnp.pad(_block_diag_query(q_a, NSA_KV), ((0, 0), (0, 0), (0, NSA_KV * HEAD_DIM)))
    zeros = jnp.zeros_like(qm_kv)
    qm_cmp = jnp.concatenate([jnp.concatenate([qm_kv, zeros], -1), jnp.concatenate([zeros, qm_kv], -1)], axis=1)
    o_cmp, idx = nsa_sample_select(page_table, qm_cmp, c_cmp.reshape(n_phys, -1))
    gate3 = jnp.pad(gate[:, :N_GATE].reshape(bsz, NSA_HEADS, 3), ((0, 0), (0, 0), (0, 125)))
    c_win_t = jnp.transpose(c_win, (0, 2, 3, 4, 1)).reshape(bsz, width, -1)
    o_a, s_win_t = nsa_sample_attend(page_table, idx[:, :NSA_KV, :N_SEL_OLD].reshape(bsz, -1), q_a, gate3, o_cmp,
                                     kv_slc.reshape(bsz, 1, width), kv_win.reshape(bsz, 1, width), kvw_t,
                                     c_win_t, _pages_feature_major(c_slc))
    q_lanes = jnp.broadcast_to((q_b * SCALE)[:, :, None], (bsz, SB_HEADS * HEAD_DIM, PAGE_SIZE))
    o_b = sb_sample(page_table, q_lanes, _pages_feature_major(c_sb))
    o_ts = jnp.concatenate([o_a.reshape(bsz, -1), o_b.reshape(bsz, -1)], axis=1).T[None].astype(BF16)
    shape5 = (bsz, 1, 2, NSA_KV, HEAD_DIM)
    s_win = jnp.transpose(s_win_t.reshape(bsz, 2, NSA_KV, HEAD_DIM, -1), (0, 4, 1, 2, 3))
    return (o_ts, kv_slc.reshape(shape5), kv_cmp.reshape(shape5), s_win,
            kv_b.reshape(bsz, 1, 2, SB_HEADS, HEAD_DIM))


def c_sample_mix(y, g_mix, w_std, c_moba, page_table):
    bsz = y.shape[0]
    q, kv = rms_proj(y, g_mix, w_std, ((0, 1024, F32), (1024, 512, F32)), None, (), 1, bsz, bsz)
    qm = _block_diag_query(q.reshape(bsz, MOBA_HEADS, HEAD_DIM), MOBA_KV)
    pages = _pages_feature_major(c_moba)
    idx = moba_sample_select(page_table, qm, pages)[:, :, :MOBA_TOPK].reshape(bsz, -1)
    hd = MOBA_HEADS * HEAD_DIM
    q_lanes = jnp.broadcast_to((q * SCALE)[:, :, None], (bsz, hd, PAGE_SIZE))
    per_head = lambda x: jnp.broadcast_to(x.reshape(bsz, MOBA_KV, 1, HEAD_DIM),
                                          (bsz, MOBA_KV, MOBA_GROUP, HEAD_DIM)).reshape(bsz, hd, 1)
    kw = MOBA_KV * HEAD_DIM
    o = moba_sample_attend(page_table, idx, q_lanes, per_head(kv[:, :kw]), per_head(kv[:, kw:]), pages)
    return o.reshape(bsz, hd).T[None].astype(BF16), kv.reshape(bsz, 1, 2, MOBA_KV, HEAD_DIM)


def kernel(x_prompt, x_sample, cache_nsa_slc, cache_nsa_cmp, cache_nsa_win, cache_sb, cache_moba, page_table,
           norm_mix, w_in_ab, w_out_ab, cmp_pos, cmp_w1, cmp_w2, w_in_c, w_out_c,
           norm_ffn, w_up, w_down, norm_final):
    bsz, seq, d = x_prompt.shape
    tm = 512
    w_up_b = w_up.astype(BF16)
    w_down_b = w_down.astype(BF16)
    dbsz = x_sample.shape[0]
    yp = x_prompt.reshape(bsz * seq, d)
    ys = x_sample.reshape(dbsz, d)
    w_oab = w_out_ab[0].astype(BF16)
    w_oc = w_out_c[0].astype(BF16)
    wc = w_in_c[0]
    k_lo, k_hi = MOBA_HEADS * HEAD_DIM, (MOBA_HEADS + MOBA_KV) * HEAD_DIM

    ab_std, ab_t = _ab_prompt_weights(w_in_ab[0])
    o_ts, p_slc, p_cmp, p_win, p_sb = ab_prompt_mix(yp, norm_mix[0], ab_std, ab_t, cmp_pos[0], cmp_w1[0], cmp_w2[0],
                                                    bsz, seq, tm)
    yp = mix_mlp(yp, o_ts, w_oab, norm_ffn[0], w_up_b[0], w_down_b[0], norm_final, False, bsz, seq, tm)
    o_ts, p_moba = c_prompt_mix(yp, norm_mix[1], wc[:, k_lo:k_hi].astype(BF16), wc.T.astype(BF16), bsz, seq, tm)
    yp = mix_mlp(yp, o_ts, w_oc, norm_ffn[1], w_up_b[1], w_down_b[1], norm_final, True, bsz, seq, tm)

    abs_std, abs_t = _ab_sample_weights(w_in_ab[0])
    o_ts, s_slc, s_cmp, s_win, s_sb = ab_sample_mix(ys, norm_mix[0], abs_std, abs_t, cache_nsa_slc[0],
                                                    cache_nsa_cmp[0], cache_nsa_win[0], cache_sb[0], page_table)
    ys = mix_mlp(ys, o_ts, w_oab, norm_ffn[0], w_up_b[0], w_down_b[0], norm_final, False, 1, dbsz, dbsz)
    o_ts, s_moba = c_sample_mix(ys, norm_mix[1], wc.astype(BF16), cache_moba[0], page_table)
    ys = mix_mlp(ys, o_ts, w_oc, norm_ffn[1], w_up_b[1], w_down_b[1], norm_final, True, 1, dbsz, dbsz)

    return (yp.reshape(bsz, seq, d), ys.reshape(dbsz, 1, d), p_slc[None], p_cmp[None], p_win[None], p_sb[None],
            p_moba[None], s_slc[None], s_cmp[None], s_win[None], s_sb[None], s_moba[None])
```

```python
import functools

import jax
import jax.numpy as jnp
from jax import lax
from jax.experimental import pallas as pl
from jax.experimental.pallas import tpu as pltpu

HEAD_DIM = 64
NSA_HEADS = 8
NSA_KV = 2
NSA_GROUP = NSA_HEADS // NSA_KV
CMP_BLOCK = 64
SEL_TOPN = 16
WINDOW = 512
SB_HEADS = 8
MOBA_HEADS = 16
MOBA_KV = 4
MOBA_GROUP = MOBA_HEADS // MOBA_KV
MOBA_BLOCK = 256
MOBA_TOPK = 3
PAGE_SIZE = 128
EPS = 1e-6
NEG = -1e30
MASK_BIAS = -(2.0 ** 100)
FORCE = 1e4
SCALE = HEAD_DIM ** -0.5

F32 = jnp.float32
BF16 = jnp.bfloat16
HI = lax.Precision.HIGHEST

VMEM_LIMIT = 48 * 1024 * 1024

NSA_TQ = 128
NSA_TK = 256
SB_T = 256
SB_CUTOFF = -110.0
EXP_CUTOFF = -110.0
NORM_SLACK = 1.02
AUG = 16
N_GATE = 3 * NSA_HEADS


def _cparams(sem):
    return pltpu.CompilerParams(dimension_semantics=sem, vmem_limit_bytes=VMEM_LIMIT)


def _alibi_slopes(n):
    return [2.0 ** (-8.0 * (h + 1) / n) for h in range(n)]


def _nt(a, b, precision=None):
    return lax.dot_general(a, b, (((1,), (1,)), ((), ())), precision=precision, preferred_element_type=F32)


def _tn(a, b):
    return lax.dot_general(a, b, (((0,), (0,)), ((), ())), preferred_element_type=F32)


def _dot(a, b, precision=None):
    return jnp.dot(a, b, precision=precision, preferred_element_type=F32)


def _rms(x, g):
    return x * lax.rsqrt(jnp.mean(x * x, axis=-1, keepdims=True) + EPS) * g


def _softplus(z):
    return jnp.maximum(z, 0.0) + jnp.log(1.0 + jnp.exp(-jnp.abs(z)))


def _iota(shape, axis):
    return lax.broadcasted_iota(jnp.int32, shape, axis)


def _proj_kernel(*refs, std, tr):
    x_ref, g_ref = refs[:2]
    refs = refs[2:]
    ws_ref = wt_ref = None
    if std:
        ws_ref, refs = refs[0], refs[1:]
    if tr:
        wt_ref, refs = refs[0], refs[1:]
    h = _rms(x_ref[...], g_ref[...]).astype(BF16)
    k = 0
    for off, wd, _ in std:
        refs[k][...] = _dot(h, ws_ref[:, off:off + wd]).astype(refs[k].dtype)
        k += 1
    for off, rows, _ in tr:
        refs[k][...] = _nt(wt_ref[off:off + rows, :], h).astype(refs[k].dtype)
        k += 1


def rms_proj(x, g, w_std, std, w_t, tr, bsz, seq, tm):
    m, d = x.shape
    nj = seq // tm
    assert m == bsz * seq and seq % tm == 0
    const = lambda i: (0, 0)
    operands = [x, g.reshape(1, d)]
    in_specs = [pl.BlockSpec((tm, d), lambda i: (i, 0)), pl.BlockSpec((1, d), const)]
    if std:
        operands.append(w_std)
        in_specs.append(pl.BlockSpec(w_std.shape, const))
    if tr:
        operands.append(w_t)
        in_specs.append(pl.BlockSpec(w_t.shape, const))
    out_specs = ([pl.BlockSpec((tm, wd), lambda i: (i, 0)) for _, wd, _ in std]
                 + [pl.BlockSpec((None, rows, tm), lambda i: (i // nj, 0, i % nj)) for _, rows, _ in tr])
    out_shape = ([jax.ShapeDtypeStruct((m, wd), dt) for _, wd, dt in std]
                 + [jax.ShapeDtypeStruct((bsz, rows, seq), dt) for _, rows, dt in tr])
    return pl.pallas_call(
        functools.partial(_proj_kernel, std=tuple(std), tr=tuple(tr)),
        grid=(m // tm,), in_specs=in_specs, out_specs=out_specs, out_shape=out_shape,
        compiler_params=_cparams(("parallel",)), name="rms_proj",
    )(*operands)


def _mix_mlp_kernel(y_ref, o_ref, wo_ref, g_ref, wu_ref, wd_ref, gf_ref, out_ref, *, ff_chunk, final):
    y = y_ref[...] + _tn(o_ref[...], wo_ref[...])
    h = _rms(y, g_ref[...]).astype(BF16)
    acc = y
    for c in range(0, wu_ref.shape[1], ff_chunk):
        a = jnp.maximum(_dot(h, wu_ref[:, c:c + ff_chunk]), 0.0)
        acc = acc + _dot((a * a).astype(BF16), wd_ref[c:c + ff_chunk, :])
    if final:
        acc = _rms(acc, gf_ref[...])
    out_ref[...] = acc


def mix_mlp(y, o_t, w_out, g_ffn, w_up, w_down, g_final, final, bsz, seq, tm):
    m, d = y.shape
    f = w_up.shape[1]
    nj = seq // tm
    const = lambda i: (0, 0)
    return pl.pallas_call(
        functools.partial(_mix_mlp_kernel, ff_chunk=min(f, 1024), final=final),
        grid=(m // tm,),
        in_specs=[pl.BlockSpec((tm, d), lambda i: (i, 0)),
                  pl.BlockSpec((None, o_t.shape[1], tm), lambda i: (i // nj, 0, i % nj)),
                  pl.BlockSpec((d, d), const), pl.BlockSpec((1, d), const), pl.BlockSpec((d, f), const),
                  pl.BlockSpec((f, d), const), pl.BlockSpec((1, d), const)],
        out_specs=pl.BlockSpec((tm, d), lambda i: (i, 0)),
        out_shape=jax.ShapeDtypeStruct((m, d), F32),
        compiler_params=_cparams(("parallel",)), name="mix_mlp",
    )(y, o_t, w_out, g_ffn.reshape(1, d), w_up, w_down, g_final.reshape(1, d))


def _compress_kernel(x_ref, pos_ref, w1_ref, w2_ref, o_ref):
    x = (x_ref[...] + pos_ref[...]).astype(BF16)
    hid = _dot(x, w1_ref[...])
    hid = hid * (1.0 / (1.0 + jnp.exp(-hid)))
    o_ref[...] = _dot(hid.astype(BF16), w2_ref[...])


def compress_blocks(x, pos, w1_bf16, w2_bf16, tr):
    _, r, fdim = x.shape
    hdim = w1_bf16.shape[2]
    return pl.pallas_call(
        _compress_kernel,
        grid=(2, r // tr),
        in_specs=[pl.BlockSpec((None, tr, fdim), lambda k, i: (k, i, 0)),
                  pl.BlockSpec((None, 1, fdim), lambda k, i: (k, 0, 0)),
                  pl.BlockSpec((None, fdim, hdim), lambda k, i: (k, 0, 0)),
                  pl.BlockSpec((None, hdim, HEAD_DIM), lambda k, i: (k, 0, 0))],
        out_specs=pl.BlockSpec((None, tr, HEAD_DIM), lambda k, i: (k, i, 0)),
        out_shape=jax.ShapeDtypeStruct((2, r, HEAD_DIM), F32),
        compiler_params=_cparams(("parallel", "parallel")), name="nsa_compress",
    )(x, pos, w1_bf16, w2_bf16)


def _biased_update(carry, s, shift, vt):
    m, l, acc = carry
    m_new = jnp.maximum(m, jnp.max(s, axis=0, keepdims=True) + shift)
    alpha = jnp.exp(m - m_new)
    p = jnp.exp(s - (m_new - shift))
    l = alpha * l + jnp.sum(p, axis=0, keepdims=True)
    acc = alpha * acc + _dot(vt, p.astype(BF16))
    return m_new, l, acc


def _slope_rows(slope, lanes):
    p1 = slope.astype(BF16).astype(F32)
    p2 = (slope - p1).astype(BF16).astype(F32)
    p3 = slope - p1 - p2
    return jnp.concatenate([p1, p2, p3, jnp.zeros((5, lanes), F32)], axis=0)


def _softmax_init(lanes):
    return (jnp.full((1, lanes), NEG, F32), jnp.zeros((1, lanes), F32), jnp.zeros((HEAD_DIM, lanes), F32))


def _join(chains):
    return tuple(jnp.concatenate(parts, axis=1) for parts in zip(*chains))


def _split(joined, n):
    w = joined[0].shape[1] // n
    return [tuple(x[:, r * w:(r + 1) * w] for x in joined) for r in range(n)]


def _nsa_prompt_kernel(qt_ref, gate_ref, kc_ref, vct_ref, ks_ref, vst_ref, kw_ref, vwt_ref, slope_ref,
                       o_ref, selneg_ref, kmax_ref, *, nb):
    g = pl.program_id(1)
    i = pl.program_id(2)
    tq, tk = NSA_TQ, NSA_TK
    lanes = NSA_GROUP * tq
    blocks_per_chunk = tk // CMP_BLOCK
    qt = jnp.concatenate([qt_ref[r * HEAD_DIM:(r + 1) * HEAD_DIM, :] for r in range(NSA_GROUP)], axis=1)
    qs = qt * SCALE
    rowgrp = _iota((NSA_KV * HEAD_DIM, 1), 0) // HEAD_DIM
    qpad = jnp.where(rowgrp == g, jnp.concatenate([qs] * NSA_KV, axis=0), 0.0).astype(BF16)
    slope = slope_ref[...]
    t_row = i * tq + _iota((1, lanes), 1) % tq
    tf = t_row.astype(F32)

    cend = _iota((nb, 1), 0) * CMP_BLOCK + (CMP_BLOCK - 1)
    sc = _dot(kc_ref[...], qs, HI) - slope * (tf - cend.astype(F32))
    vis = cend <= t_row
    sc = jnp.where(vis, sc, NEG)
    e = jnp.where(vis, jnp.exp(sc - jnp.max(sc, axis=0, keepdims=True)), 0.0)
    den = jnp.sum(e, axis=0, keepdims=True)
    p = e * jnp.where(den > 0.0, 1.0 / den, 0.0)
    o_cmp = _dot(vct_ref[...], p, HI)

    imp = p[:, 0:tq]
    for r in range(1, NSA_GROUP):
        imp = imp + p[:, r * tq:(r + 1) * tq]
    j = _iota((nb, tq), 0)
    own = (i * tq + _iota((nb, tq), 1)) // CMP_BLOCK
    forced = (j == 0) | (j == own) | (j == own - 1)
    score = jnp.where(forced, FORCE, imp)
    score = jnp.where(j <= own, score, -1.0)
    rank = jnp.zeros((nb, tq), F32)
    for m in range(nb):
        sm = score[m:m + 1, :]
        beats = (sm > score) | ((sm == score) & (j > m))
        rank = rank + jnp.where(beats, 1.0, 0.0)
    selneg = jnp.where((rank < float(min(SEL_TOPN, nb))) & (score >= 0.0), 0.0, MASK_BIAS)
    for c in range(nb // blocks_per_chunk):
        selneg_ref[c] = jnp.concatenate([selneg[c * blocks_per_chunk:(c + 1) * blocks_per_chunk],
                                         jnp.zeros((8 - blocks_per_chunk, tq), F32)], axis=0)

    sub = _iota((tk, 1), 0)
    col = _iota((tk, AUG), 1)
    aug = jnp.where(col < blocks_per_chunk, jnp.where(sub // CMP_BLOCK == col, 1.0, 0.0),
                    jnp.where((col >= 8) & (col < 11), sub.astype(F32), 0.0)).astype(BF16)
    slope8 = _slope_rows(slope, lanes)
    cd = (i * tq) // tk
    rel = (i * tq - cd * tk) + _iota((1, lanes), 1) % tq
    zeros8 = jnp.zeros((8, lanes), F32)
    causal = jnp.where(sub <= rel, 0.0, MASK_BIAS)
    win_tail = jnp.where((sub > rel) & (cd >= 2), 0.0, MASK_BIAS)

    def scores(k_ref, c, mask_rows):
        ks = pl.multiple_of(c * tk, tk)
        bias_rows = jnp.concatenate([mask_rows, slope8], axis=0).astype(BF16)
        return _dot(k_ref[pl.ds(ks, tk), :], qpad) + _dot(aug, bias_rows)

    def update(carry, s, vt_ref, c):
        ks = pl.multiple_of(c * tk, tk)
        return _biased_update(carry, s, slope * ((c * tk).astype(F32) - tf), vt_ref[:, pl.ds(ks, tk)].astype(BF16))

    def sel_rows(c):
        return jnp.concatenate([selneg_ref[c]] * NSA_GROUP, axis=1)

    init = _softmax_init(lanes)
    carry = update(init, scores(ks_ref, cd, sel_rows(cd)) + causal, vst_ref, cd)

    def slc_body(c, state):
        carry, s = state
        nxt = jnp.minimum(c + 1, jnp.maximum(cd - 1, 0))
        s_next = scores(ks_ref, nxt, sel_rows(nxt))
        return update(carry, s, vst_ref, c), s_next

    @pl.when(i == 0)
    def _():
        k32 = ks_ref[...].astype(F32)
        own = _iota((1, NSA_KV * HEAD_DIM), 1) // HEAD_DIM == g
        norm2 = jnp.sum(jnp.where(own, k32 * k32, 0.0), axis=1, keepdims=True)
        kmax_ref[...] = jnp.broadcast_to(jnp.sqrt(jnp.max(norm2, axis=0, keepdims=True)), kmax_ref.shape)

    bound = jnp.sqrt(jnp.sum(qs * qs, axis=0, keepdims=True)) * kmax_ref[:, 0:1] * NORM_SLACK
    reach = (bound - carry[0] - EXP_CUTOFF) / slope
    need = jnp.clip(jnp.ceil((reach - 1.0) * (1.0 / tk)), 0.0, float(nb))
    c_lo = jnp.maximum(cd - jnp.max(need, axis=1, keepdims=True).astype(jnp.int32)[0, 0], 0)
    c_first = jnp.minimum(c_lo, jnp.maximum(cd - 1, 0))
    (_, l_s, acc_s), _ = lax.fori_loop(c_lo, cd, slc_body, (carry, scores(ks_ref, c_first, sel_rows(c_first))))

    c1, c2 = jnp.maximum(cd - 1, 0), jnp.maximum(cd - 2, 0)
    s0 = scores(kw_ref, cd, zeros8) + causal
    s1 = scores(kw_ref, c1, jnp.where(cd >= 1, zeros8, MASK_BIAS))
    s2 = scores(kw_ref, c2, zeros8) + win_tail
    _, l_w, acc_w = update(update(update(init, s0, vwt_ref, cd), s1, vwt_ref, c1), s2, vwt_ref, c2)

    gates = []
    for c in range(3):
        row = jnp.concatenate([gate_ref[c * NSA_GROUP + r:c * NSA_GROUP + r + 1, :] for r in range(NSA_GROUP)],
                              axis=1)
        gates.append(1.0 / (1.0 + jnp.exp(-row)))
    o = gates[0] * o_cmp + gates[1] * (acc_s / l_s) + gates[2] * (acc_w / l_w)
    for r in range(NSA_GROUP):
        o_ref[r * HEAD_DIM:(r + 1) * HEAD_DIM, :] = o[:, r * tq:(r + 1) * tq].astype(o_ref.dtype)


def nsa_prompt(qa_t, gate_t, kc, vct, ks, kvs_t, kw, kvw_t, slopes):
    bsz, _, seq = qa_t.shape
    nb = kc.shape[2]
    tq = NSA_TQ
    lanes = NSA_GROUP * tq
    gw = NSA_GROUP * HEAD_DIM
    assert seq % NSA_TK == 0 and WINDOW == 2 * NSA_TK and NSA_TK % NSA_TQ == 0
    return pl.pallas_call(
        functools.partial(_nsa_prompt_kernel, nb=nb),
        grid=(bsz, NSA_KV, seq // tq),
        in_specs=[pl.BlockSpec((None, gw, tq), lambda b, g, i: (b, g, i)),
                  pl.BlockSpec((None, 16, tq), lambda b, g, i: (b, g, i)),
                  pl.BlockSpec((None, None, nb, HEAD_DIM), lambda b, g, i: (b, g, 0, 0)),
                  pl.BlockSpec((None, None, HEAD_DIM, nb), lambda b, g, i: (b, g, 0, 0)),
                  pl.BlockSpec((None, seq, NSA_KV * HEAD_DIM), lambda b, g, i: (b, 0, 0)),
                  pl.BlockSpec((None, HEAD_DIM, seq), lambda b, g, i: (b, NSA_KV + g, 0)),
                  pl.BlockSpec((None, seq, NSA_KV * HEAD_DIM), lambda b, g, i: (b, 0, 0)),
                  pl.BlockSpec((None, HEAD_DIM, seq), lambda b, g, i: (b, NSA_KV + g, 0)),
                  pl.BlockSpec((None, 1, lanes), lambda b, g, i: (g, 0, 0))],
        out_specs=pl.BlockSpec((None, gw, tq), lambda b, g, i: (b, g, i)),
        out_shape=jax.ShapeDtypeStruct((bsz, NSA_HEADS * HEAD_DIM, seq), BF16),
        scratch_shapes=[pltpu.VMEM((nb * CMP_BLOCK // NSA_TK, 8, tq), F32), pltpu.VMEM((8, 128), F32)],
        compiler_params=_cparams(("parallel", "parallel", "arbitrary")), name="nsa_prompt",
    )(qa_t, gate_t, kc, vct, ks, kvs_t, kw, kvw_t, slopes)


def _sb_prompt_kernel(q_ref, k_ref, vt_ref, o_ref):
    i = pl.program_id(2)
    t = SB_T
    qpair = (q_ref[...].astype(F32) * SCALE).astype(BF16)
    rowh = _iota((2 * HEAD_DIM, 1), 0) // HEAD_DIM
    qcat = jnp.concatenate([jnp.where(rowh == hh, qpair, jnp.zeros_like(qpair)) for hh in range(2)], axis=1)
    th = t // 2
    after = jnp.where(_iota((th, th), 0) < _iota((th, th), 1), 1.0, 0.0).astype(BF16)
    vis = _iota((t, 1), 0) < _iota((1, 2 * t), 1) % t

    def logits(kb):
        return _dot(k_ref[pl.ds(pl.multiple_of(kb * t, t), t), :], qcat)

    def step(kb, carry, z, diagonal):
        acc, run = carry
        ks = pl.multiple_of(kb * t, t)
        sp = _softplus(z)
        ls = -sp
        if diagonal:
            ls = jnp.where(vis, ls, 0.0)
        hi = ls.astype(BF16)
        lo = (ls - hi.astype(F32)).astype(BF16)
        later = []
        for half in (1, 0):
            rows = slice(half * th, (half + 1) * th)
            later.insert(0, _dot(after, hi[rows]) + _dot(after, lo[rows]) + run)
            run = run + jnp.sum(ls[rows], axis=0, keepdims=True)
        w = jnp.exp(z - sp + jnp.concatenate(later, axis=0))
        if diagonal:
            w = jnp.where(vis, w, 0.0)
        w = w.astype(BF16)
        pv = [_dot(vt_ref[hh * HEAD_DIM:(hh + 1) * HEAD_DIM, pl.ds(ks, t)].astype(BF16), w[:, hh * t:(hh + 1) * t])
              for hh in range(2)]
        return acc + jnp.concatenate(pv, axis=1), run

    def body(state):
        st, carry, z = state
        kb = i - st
        z_next = logits(jnp.maximum(kb - 1, 0))
        return st + 1, step(kb, carry, z, False), z_next

    def more(state):
        st, (_, run), _ = state
        return (st <= i) & (jnp.max(run) > SB_CUTOFF)

    carry = step(i, (jnp.zeros((HEAD_DIM, 2 * t), F32), jnp.zeros((1, 2 * t), F32)), logits(i), True)
    _, (acc, _), _ = lax.while_loop(more, body, (jnp.int32(1), carry, logits(jnp.maximum(i - 1, 0))))
    for hh in range(2):
        o_ref[hh * HEAD_DIM:(hh + 1) * HEAD_DIM, :] = acc[:, hh * t:(hh + 1) * t].astype(o_ref.dtype)


def sb_prompt(qb_t, kb, kvb_t):
    bsz, width, seq = qb_t.shape
    pw = 2 * HEAD_DIM
    n_pairs = width // pw
    return pl.pallas_call(
        _sb_prompt_kernel,
        grid=(bsz, n_pairs, seq // SB_T),
        in_specs=[pl.BlockSpec((None, pw, SB_T), lambda b, p, i: (b, p, i)),
                  pl.BlockSpec((None, seq, pw), lambda b, p, i: (b, 0, p)),
                  pl.BlockSpec((None, pw, seq), lambda b, p, i: (b, n_pairs + p, 0))],
        out_specs=pl.BlockSpec((None, pw, SB_T), lambda b, p, i: (b, p, i)),
        out_shape=jax.ShapeDtypeStruct((bsz, width, seq), BF16),
        compiler_params=_cparams(("parallel", "parallel", "arbitrary")), name="sb_prompt",
    )(qb_t, kb, kvb_t)


def _block_mean_kernel(k_ref, o_ref, n2_ref):
    k = k_ref[...]
    o_ref[...] = jnp.sum(k, axis=0, keepdims=True) * (1.0 / MOBA_BLOCK)
    sq = k * k
    rows = [jnp.broadcast_to(jnp.max(jnp.sum(sq[:, g * HEAD_DIM:(g + 1) * HEAD_DIM], axis=1, keepdims=True),
                                     axis=0, keepdims=True), (1, HEAD_DIM)) for g in range(MOBA_KV)]
    n2_ref[...] = jnp.concatenate(rows, axis=1)


def moba_block_means(k, bsz, seq):
    nblk = seq // MOBA_BLOCK
    width = k.shape[1]
    out = pl.BlockSpec((None, None, 1, width), lambda b, n: (b, n, 0, 0))
    return pl.pallas_call(
        _block_mean_kernel,
        grid=(bsz, nblk),
        in_specs=[pl.BlockSpec((MOBA_BLOCK, width), lambda b, n: (b * nblk + n, 0))],
        out_specs=[out, out],
        out_shape=[jax.ShapeDtypeStruct((bsz, nblk, 1, width), F32)] * 2,
        compiler_params=_cparams(("parallel", "parallel")), name="moba_block_means",
    )(k)


def _moba_prompt_kernel(qt_ref, means_ref, kmax_ref, k_ref, vt_ref, slope_ref, o_ref, selneg_ref, *, nblk):
    g = pl.program_id(1)
    i = pl.program_id(2)
    tq = MOBA_BLOCK
    lanes = MOBA_GROUP * tq
    qt = jnp.concatenate([qt_ref[r * HEAD_DIM:(r + 1) * HEAD_DIM, :] for r in range(MOBA_GROUP)], axis=1)
    qs = qt * SCALE
    rowgrp = _iota((MOBA_KV * HEAD_DIM, 1), 0) // HEAD_DIM
    qpad = jnp.where(rowgrp == g, jnp.concatenate([qs] * MOBA_KV, axis=0), 0.0).astype(BF16)
    slope = slope_ref[...]
    t_row = i * tq + _iota((1, lanes), 1) % tq
    tf = t_row.astype(F32)

    gsc = _dot(means_ref[...], qt, HI)
    n_idx = _iota((nblk, lanes), 0)
    past = n_idx < i
    gsc = jnp.where(past, gsc, NEG)
    rank = jnp.zeros((nblk, lanes), F32)
    for m in range(nblk):
        gm = gsc[m:m + 1, :]
        beats = (gm > gsc) | ((gm == gsc) & (n_idx > m))
        rank = rank + jnp.where(beats, 1.0, 0.0)
    selneg_ref[...] = jnp.where((rank < float(min(MOBA_TOPK, nblk))) & past, 0.0, MASK_BIAS)

    sub = _iota((tq, 1), 0)
    ramp = slope * sub.astype(F32)

    def scores(n):
        return _dot(k_ref[pl.ds(pl.multiple_of(n * tq, tq), tq), :], qpad) + ramp

    def update(carry, s, n, mask_row):
        shift = slope * ((n * tq).astype(F32) - tf) + mask_row
        return _biased_update(carry, s, shift, vt_ref[:, pl.ds(pl.multiple_of(n * tq, tq), tq)].astype(BF16))

    causal = jnp.where(sub <= _iota((1, lanes), 1) % tq, 0.0, MASK_BIAS)
    carry = update(_softmax_init(lanes), scores(i) + causal, i, 0.0)

    bound = jnp.sqrt(jnp.sum(qs * qs, axis=0, keepdims=True)) * kmax_ref[:, 0:1] * NORM_SLACK
    reach = (bound - carry[0] - EXP_CUTOFF) / slope
    need = jnp.clip(jnp.ceil((reach - 1.0) * (1.0 / tq)), 0.0, float(nblk))
    n_lo = jnp.maximum(i - jnp.max(need, axis=1, keepdims=True).astype(jnp.int32)[0, 0], 0)

    def body(n, state):
        carry, s = state
        s_next = scores(jnp.minimum(n + 1, jnp.maximum(i - 1, 0)))
        return update(carry, s, n, selneg_ref[pl.ds(n, 1), :]), s_next

    (_, l, acc), _ = lax.fori_loop(n_lo, i, body, (carry, scores(jnp.minimum(n_lo, jnp.maximum(i - 1, 0)))))
    o = acc / l
    for r in range(MOBA_GROUP):
        o_ref[r * HEAD_DIM:(r + 1) * HEAD_DIM, :] = o[:, r * tq:(r + 1) * tq].astype(o_ref.dtype)


def moba_prompt(q_t, means, kmax, k, kv_t, slopes):
    bsz, _, seq = q_t.shape
    nblk = seq // MOBA_BLOCK
    gw = MOBA_GROUP * HEAD_DIM
    lanes = MOBA_GROUP * MOBA_BLOCK
    return pl.pallas_call(
        functools.partial(_moba_prompt_kernel, nblk=nblk),
        grid=(bsz, MOBA_KV, nblk),
        in_specs=[pl.BlockSpec((None, gw, MOBA_BLOCK), lambda b, g, i: (b, g, i)),
                  pl.BlockSpec((None, None, nblk, HEAD_DIM), lambda b, g, i: (b, g, 0, 0)),
                  pl.BlockSpec((None, None, 1, 128), lambda b, g, i: (b, g, 0, 0)),
                  pl.BlockSpec((None, seq, MOBA_KV * HEAD_DIM), lambda b, g, i: (b, 0, 0)),
                  pl.BlockSpec((None, HEAD_DIM, seq), lambda b, g, i: (b, MOBA_KV + g, 0)),
                  pl.BlockSpec((None, 1, lanes), lambda b, g, i: (g, 0, 0))],
        out_specs=pl.BlockSpec((None, gw, MOBA_BLOCK), lambda b, g, i: (b, g, i)),
        out_shape=jax.ShapeDtypeStruct((bsz, MOBA_HEADS * HEAD_DIM, seq), BF16),
        scratch_shapes=[pltpu.VMEM((nblk, lanes), F32)],
        compiler_params=_cparams(("parallel", "parallel", "arbitrary")), name="moba_prompt",
    )(q_t, means, kmax, k, kv_t, slopes)


AB_COLS = dict(q_a=(0, 512), cmp=(512, 768), slc=(768, 1024), win=(1024, 1280), gate=(1280, 1304),
               q_b=(1304, 1816), kv_b=(1816, 2840))
AB_WIDTHS = (512, 256, 256, 256, 128, 512, 1024)


def _cols(w, name, lo=0, hi=None):
    a, b = AB_COLS[name]
    return w[:, a + lo:(b if hi is None else a + hi)]


def _ab_prompt_weights(w):
    w_std = jnp.concatenate([_cols(w, "cmp"), _cols(w, "slc", 0, 128), _cols(w, "win", 0, 128),
                             _cols(w, "kv_b", 0, 512)], axis=1).astype(BF16)
    wt = w.T
    g0 = AB_COLS["gate"][0]
    idx, keep = [], []
    for g in range(NSA_KV):
        for c in range(4):
            for r in range(NSA_GROUP):
                idx.append(g0 + (g * NSA_GROUP + r) * 3 + min(c, 2))
                keep.append(1.0 if c < 3 else 0.0)
    gate_rows = wt[jnp.asarray(idx)] * jnp.asarray(keep, F32)[:, None]
    rows = lambda name: wt[AB_COLS[name][0]:AB_COLS[name][1]]
    w_t = jnp.concatenate([rows("q_a"), rows("slc"), rows("win"), gate_rows, rows("q_b"), rows("kv_b")],
                          axis=0).astype(BF16)
    return w_std, w_t


AB_STD = ((0, 256, F32), (256, 128, BF16), (384, 128, BF16), (512, 512, BF16))
AB_TR = ((0, 512, F32), (512, 256, F32), (768, 256, F32), (1024, 32, F32), (1056, 512, BF16), (1568, 1024, F32))


def _group_slopes(n_heads, n_kv, tq):
    sl = jnp.asarray(_alibi_slopes(n_heads), F32).reshape(n_kv, n_heads // n_kv, 1)
    return jnp.broadcast_to(sl, (n_kv, n_heads // n_kv, tq)).reshape(n_kv, 1, (n_heads // n_kv) * tq)


def _cache_view(x_t, n_heads):
    bsz, _, seq = x_t.shape
    return jnp.transpose(x_t.reshape(bsz, 2, n_heads, HEAD_DIM, seq), (0, 4, 1, 2, 3))


def ab_prompt_mix(y, g_mix, w_std, w_t, c_pos, c_w1, c_w2, bsz, seq, tm):
    kv_cmp, ks, kw, kb, qa_t, kvs_t, kvw_t, gate_t, qb_t, kvb_t = rms_proj(
        y, g_mix, w_std, AB_STD, w_t, AB_TR, bsz, seq, tm)
    nb = seq // CMP_BLOCK

    x = jnp.transpose(kv_cmp.reshape(bsz, nb, CMP_BLOCK, 2, NSA_KV, HEAD_DIM), (3, 0, 1, 4, 2, 5))
    rows = bsz * nb * NSA_KV
    x = x.reshape(2, rows, CMP_BLOCK * HEAD_DIM)
    cmp = compress_blocks(x, c_pos.reshape(2, 1, CMP_BLOCK * HEAD_DIM), c_w1.astype(BF16), c_w2.astype(BF16),
                          min(rows, 256))
    cmp = cmp.reshape(2, bsz, nb, NSA_KV, HEAD_DIM)
    p_cmp = jnp.transpose(cmp, (1, 2, 0, 3, 4))
    kc = jnp.transpose(cmp[0], (0, 2, 1, 3))
    vct = jnp.transpose(cmp[1], (0, 2, 3, 1))

    kw_full = NSA_KV * HEAD_DIM
    oa_t = nsa_prompt(qa_t, gate_t, kc, vct, ks.reshape(bsz, seq, kw_full), kvs_t, kw.reshape(bsz, seq, kw_full),
                      kvw_t, _group_slopes(NSA_HEADS, NSA_KV, NSA_TQ))
    ob_t = sb_prompt(qb_t, kb.reshape(bsz, seq, SB_HEADS * HEAD_DIM), kvb_t)
    wlen = min(WINDOW, seq)
    return (jnp.concatenate([oa_t, ob_t], axis=1), _cache_view(kvs_t, NSA_KV), p_cmp,
            _cache_view(kvw_t[:, :, seq - wlen:], NSA_KV),
            _cache_view(kvb_t, SB_HEADS))


C_STD = ((0, 256, F32), (0, 256, BF16))
C_TR = ((0, 1024, F32), (1024, 512, F32))


def c_prompt_mix(y, g_mix, w_std, w_t, bsz, seq, tm):
    k32, k16, q_t, kv_t = rms_proj(y, g_mix, w_std, C_STD, w_t, C_TR, bsz, seq, tm)
    nblk = seq // MOBA_BLOCK
    means, norm2 = moba_block_means(k32, bsz, seq)
    means = jnp.transpose(means.reshape(bsz, nblk, MOBA_KV, HEAD_DIM), (0, 2, 1, 3))
    kmax = jnp.sqrt(jnp.max(norm2.reshape(bsz, nblk, MOBA_KV, HEAD_DIM)[..., 0], axis=1))
    kmax = jnp.broadcast_to(kmax[:, :, None, None], (bsz, MOBA_KV, 1, 128))
    o_t = moba_prompt(q_t, means, kmax, k16.reshape(bsz, seq, MOBA_KV * HEAD_DIM), kv_t,
                      _group_slopes(MOBA_HEADS, MOBA_KV, MOBA_BLOCK))
    return o_t, _cache_view(kv_t, MOBA_KV)


PAGES_PER_STEP = 8
N_SEL_OLD = SEL_TOPN - 1
BIG_IDX = 1 << 30


def _head_slopes(n_heads):
    return jnp.broadcast_to(jnp.asarray(_alibi_slopes(n_heads), F32).reshape(n_heads, 1), (n_heads, 128))


def _fold_heads(wide, n_groups, heads_per_group):
    n_heads = n_groups * heads_per_group
    hrow = _iota((n_heads, HEAD_DIM), 0) // heads_per_group
    out = jnp.zeros((n_heads, HEAD_DIM), F32)
    for g in range(n_groups):
        out = out + jnp.where(hrow == g, wide[:, g * HEAD_DIM:(g + 1) * HEAD_DIM], 0.0)
    return out


def _nsa_select_kernel(pt_ref, qm_ref, slope_ref, ccmp_ref, ocmp_ref, idx_ref, buf_ref, sem, *, n_pages):
    b = pl.program_id(0)
    row_copy = lambda p, page: pltpu.make_async_copy(ccmp_ref.at[pl.ds(page, 1), :], buf_ref.at[pl.ds(p, 1), :], sem)

    def issue(p, c):
        row_copy(p, pt_ref[b, p]).start()
        return c

    def drain(p, c):
        row_copy(p, 0).wait()
        return c

    lax.fori_loop(0, n_pages, issue, 0)
    lax.fori_loop(0, n_pages, drain, 0)

    buf = buf_ref[...]
    slope = slope_ref[:, 0:1]
    qpos = float(n_pages * PAGE_SIZE)
    page = _iota((1, n_pages), 1)
    st = _nt(qm_ref[...], buf, HI) * SCALE
    halves = []
    for blk in range(2):
        cend = (page * PAGE_SIZE + blk * CMP_BLOCK + (CMP_BLOCK - 1)).astype(F32)
        halves.append(st[blk * NSA_HEADS:(blk + 1) * NSA_HEADS] - slope * (qpos - cend))
    mx = jnp.maximum(jnp.max(halves[0], axis=1, keepdims=True), jnp.max(halves[1], axis=1, keepdims=True))
    e = [jnp.exp(hv - mx) for hv in halves]
    inv = 1.0 / (jnp.sum(e[0], axis=1, keepdims=True) + jnp.sum(e[1], axis=1, keepdims=True))
    p = [ev * inv for ev in e]
    v_lo = NSA_KV * HEAD_DIM
    ocmp_ref[...] = _dot(p[0], buf[:, v_lo:2 * v_lo], HI) + _dot(p[1], buf[:, 3 * v_lo:4 * v_lo], HI)

    hrow = _iota((NSA_HEADS, n_pages), 0) // NSA_GROUP
    jidx = (2 * _iota((2, n_pages), 1) + _iota((2, n_pages), 0)).astype(F32)
    lane = _iota((1, 128), 1)
    n_blk = 2 * n_pages
    rows = []
    for g in range(NSA_KV):
        imp = jnp.concatenate([jnp.sum(jnp.where(hrow == g, pv, 0.0), axis=0, keepdims=True) for pv in p], axis=0)
        cand = jnp.where((jidx == 0.0) | (jidx == float(n_blk - 1)), -1.0, imp)
        vec = jnp.where(lane == N_SEL_OLD - 1, float(n_blk - 1), 0.0)
        for r in range(N_SEL_OLD - 2):
            best = jnp.max(jnp.max(cand, axis=1, keepdims=True), axis=0, keepdims=True)
            hit = jnp.where(cand == best, jidx, float(BIG_IDX))
            pick = jnp.min(jnp.min(hit, axis=1, keepdims=True), axis=0, keepdims=True)
            vec = jnp.where(lane == r, pick, vec)
            cand = jnp.where(jidx == pick, -2.0, cand)
        rows.append(vec)
    idx_ref[...] = jnp.concatenate(rows + [jnp.zeros((8 - NSA_KV, 128), F32)], axis=0).astype(jnp.int32)


def nsa_sample_select(page_table, qm_cmp, c_cmp):
    bsz, n_pages = page_table.shape
    width = c_cmp.shape[1]
    assert 2 * n_pages - 2 >= N_SEL_OLD - 2
    return pl.pallas_call(
        functools.partial(_nsa_select_kernel, n_pages=n_pages),
        grid_spec=pltpu.PrefetchScalarGridSpec(
            num_scalar_prefetch=1,
            grid=(bsz,),
            in_specs=[pl.BlockSpec((None, 2 * NSA_HEADS, width), lambda b, pt: (b, 0, 0)),
                      pl.BlockSpec((NSA_HEADS, 128), lambda b, pt: (0, 0)),
                      pl.BlockSpec(memory_space=pl.ANY)],
            out_specs=[pl.BlockSpec((None, NSA_HEADS, 128), lambda b, pt: (b, 0, 0)),
                       pl.BlockSpec((None, 8, 128), lambda b, pt: (b, 0, 0))],
            scratch_shapes=[pltpu.VMEM((n_pages, width), F32), pltpu.SemaphoreType.DMA(())]),
        out_shape=[jax.ShapeDtypeStruct((bsz, NSA_HEADS, 128), F32),
                   jax.ShapeDtypeStruct((bsz, 8, 128), jnp.int32)],
        compiler_params=_cparams(("arbitrary",)), name="nsa_sample_select",
    )(page_table, qm_cmp, _head_slopes(NSA_HEADS), c_cmp)


def _nsa_attend_kernel(pt_ref, idx_ref, q_ref, slope_ref, gate_ref, ocmp_ref, new_slc_ref, new_win_ref,
                       new_win_t_ref, cwin_ref, cslc_ref, o_ref, swin_ref, kbuf_ref, vbuf_ref, sem, *, n_pages):
    b = pl.program_id(0)
    gw = NSA_KV * HEAD_DIM

    def page_copies(g, r):
        page = pt_ref[b, idx_ref[b, g * N_SEL_OLD + r] // 2]
        dst = pl.ds(r * PAGE_SIZE, PAGE_SIZE)
        return (pltpu.make_async_copy(cslc_ref.at[page, pl.ds(g * HEAD_DIM, HEAD_DIM), :],
                                      kbuf_ref.at[g, :, dst], sem),
                pltpu.make_async_copy(cslc_ref.at[page, pl.ds(gw + g * HEAD_DIM, HEAD_DIM), :],
                                      vbuf_ref.at[g, :, dst], sem))

    for g in range(NSA_KV):
        for r in range(N_SEL_OLD):
            for cp in page_copies(g, r):
                cp.start()
    for g in range(NSA_KV):
        for r in range(N_SEL_OLD):
            for cp in page_copies(g, r):
                cp.wait()

    q = q_ref[...]
    slope = slope_ref[:, 0:1]
    qpos = n_pages * PAGE_SIZE
    hgrp = _iota((NSA_HEADS, 1), 0) // NSA_GROUP
    n_keys = N_SEL_OLD * PAGE_SIZE
    lane = _iota((1, n_keys), 1)
    slot = lane // PAGE_SIZE
    tok = lane % PAGE_SIZE

    def per_group_rows(row):
        k = jnp.where(hgrp == 0, row[:, 0:HEAD_DIM], row[:, HEAD_DIM:2 * HEAD_DIM])
        v = jnp.where(hgrp == 0, row[:, gw:gw + HEAD_DIM], row[:, gw + HEAD_DIM:gw + 2 * HEAD_DIM])
        return k, v

    def finish(s, s_new, pv, v_new):
        mx = jnp.maximum(jnp.max(s, axis=1, keepdims=True), s_new)
        e = jnp.exp(s - mx)
        e_new = jnp.exp(s_new - mx)
        return (pv(e) + e_new * v_new) / (jnp.sum(e, axis=1, keepdims=True) + e_new)

    s_slc = jnp.zeros((NSA_HEADS, n_keys), F32)
    kpos = jnp.zeros((NSA_HEADS, n_keys), jnp.int32)
    valid = jnp.zeros((NSA_HEADS, n_keys), jnp.int32)
    for g in range(NSA_KV):
        s_slc = s_slc + _dot(jnp.where(hgrp == g, q, 0.0), kbuf_ref[g], HI)
        page_pos = jnp.zeros((1, n_keys), jnp.int32)
        half = jnp.zeros((1, n_keys), jnp.int32)
        for r in range(N_SEL_OLD):
            j = idx_ref[b, g * N_SEL_OLD + r]
            page_pos = jnp.where(slot == r, (j // 2) * PAGE_SIZE, page_pos)
            half = jnp.where(slot == r, j % 2, half)
        kpos = jnp.where(hgrp == g, page_pos + tok, kpos)
        valid = jnp.where(hgrp == g, jnp.where((tok // CMP_BLOCK) == half, 1, 0), valid)
    s_slc = jnp.where(valid > 0, s_slc * SCALE - slope * (qpos - kpos).astype(F32), NEG)
    k_new, v_new = per_group_rows(new_slc_ref[...])
    s_new = jnp.sum(q * k_new, axis=1, keepdims=True) * SCALE

    def pv_slc(e):
        return sum(_nt(jnp.where(hgrp == g, e, 0.0), vbuf_ref[g], HI) for g in range(NSA_KV))

    o_slc = finish(s_slc, s_new, pv_slc, v_new)

    cw = cwin_ref[...]
    wlen = cw.shape[1]
    wl = _iota((1, wlen), 1)
    q_groups = jnp.concatenate([jnp.where(hgrp == g, q, 0.0) for g in range(NSA_KV)], axis=1)
    s_w = _dot(q_groups, cw[0:gw], HI) * SCALE - slope * (wlen - wl).astype(F32)
    s_w = jnp.where(wl >= 1, s_w, NEG)
    k_new, v_new = per_group_rows(new_win_ref[...])
    s_new = jnp.sum(q * k_new, axis=1, keepdims=True) * SCALE
    o_win = finish(s_w, s_new, lambda e: _fold_heads(_nt(e, cw[gw:2 * gw], HI), NSA_KV, NSA_GROUP), v_new)

    new_t = new_win_t_ref[...]
    new_col = jnp.sum(jnp.where(_iota((1, new_t.shape[1]), 1) == b, new_t, 0.0), axis=1, keepdims=True)
    swin_ref[...] = jnp.where(wl == wlen - 1, new_col, pltpu.roll(cw, wlen - 1, 1))

    gate = 1.0 / (1.0 + jnp.exp(-gate_ref[...]))
    o_ref[...] = (gate[:, 0:1] * _fold_heads(ocmp_ref[...], NSA_KV, NSA_GROUP) + gate[:, 1:2] * o_slc
                  + gate[:, 2:3] * o_win)


def nsa_sample_attend(page_table, idx, q, gate, o_cmp, new_slc, new_win, new_win_t, c_win_t, c_slc_t):
    bsz, n_pages = page_table.shape
    width, wlen = c_win_t.shape[1:]
    per_b = lambda *shape: pl.BlockSpec((None,) + shape, lambda b, pt, ix: (b, 0, 0))
    n_keys = N_SEL_OLD * PAGE_SIZE
    return pl.pallas_call(
        functools.partial(_nsa_attend_kernel, n_pages=n_pages),
        grid_spec=pltpu.PrefetchScalarGridSpec(
            num_scalar_prefetch=2,
            grid=(bsz,),
            in_specs=[per_b(NSA_HEADS, HEAD_DIM),
                      pl.BlockSpec((NSA_HEADS, 128), lambda b, pt, ix: (0, 0)),
                      per_b(NSA_HEADS, 128), per_b(NSA_HEADS, 128), per_b(1, width), per_b(1, width),
                      pl.BlockSpec((None, width, bsz), lambda b, pt, ix: (0, 0, 0)),
                      per_b(width, wlen),
                      pl.BlockSpec(memory_space=pl.ANY)],
            out_specs=[per_b(NSA_HEADS, HEAD_DIM), per_b(width, wlen)],
            scratch_shapes=[pltpu.VMEM((NSA_KV, HEAD_DIM, n_keys), F32), pltpu.VMEM((NSA_KV, HEAD_DIM, n_keys), F32),
                            pltpu.SemaphoreType.DMA(())]),
        out_shape=[jax.ShapeDtypeStruct((bsz, NSA_HEADS, HEAD_DIM), F32),
                   jax.ShapeDtypeStruct((bsz, width, wlen), F32)],
        compiler_params=_cparams(("arbitrary",)), name="nsa_sample_attend",
    )(page_table, idx, q, _head_slopes(NSA_HEADS), gate, o_cmp, new_slc, new_win, new_win_t, c_win_t, c_slc_t)


def _sb_sample_kernel(pt_ref, q_ref, cache_ref, o_ref, buf_ref, sem, *, n_pages):
    b = pl.program_id(0)
    width = SB_HEADS * HEAD_DIM
    tk = PAGE_SIZE
    shape3 = (SB_HEADS, HEAD_DIM, tk)
    q3 = q_ref[...].reshape(shape3)
    lane = _iota((1, tk), 1)
    slot_of = lambda p: (n_pages - 1 - p) % 2
    page_copy = lambda p: pltpu.make_async_copy(cache_ref.at[pt_ref[b, p]], buf_ref.at[slot_of(p)],
                                                sem.at[slot_of(p)])
    page_copy(n_pages - 1).start()

    def more(state):
        p, _, run = state
        return (p >= 0) & (jnp.max(run) > SB_CUTOFF)

    def body(state):
        p, acc, run = state
        page_copy(p).wait()

        @pl.when(p >= 1)
        def _():
            page_copy(p - 1).start()

        page_ref = buf_ref.at[slot_of(p)]
        z = jnp.sum(page_ref[0:width, :].reshape(shape3) * q3, axis=1)
        sp = _softplus(z)
        ls = -sp
        suffix = ls
        for sh in (1, 2, 4, 8, 16, 32, 64):
            suffix = suffix + jnp.where(lane < tk - sh, pltpu.roll(suffix, tk - sh, 1), 0.0)
        w = jnp.exp(z - sp + (suffix - ls) + run)
        acc = acc + page_ref[width:2 * width, :].reshape(shape3) * w[:, None, :]
        return p - 1, acc, run + suffix[:, 0:1]

    p, acc, _ = lax.while_loop(more, body, (jnp.int32(n_pages - 1), jnp.zeros(shape3, F32),
                                            jnp.zeros((SB_HEADS, 1), F32)))

    @pl.when(p >= 0)
    def _():
        page_copy(p).wait()

    o_ref[...] = jnp.sum(acc.reshape(width, tk), axis=1, keepdims=True)


def sb_sample(page_table, q_lanes, c_sb_t):
    bsz, n_pages = page_table.shape
    width2 = c_sb_t.shape[1]
    return pl.pallas_call(
        functools.partial(_sb_sample_kernel, n_pages=n_pages),
        grid_spec=pltpu.PrefetchScalarGridSpec(
            num_scalar_prefetch=1,
            grid=(bsz,),
            in_specs=[pl.BlockSpec((None, width2 // 2, PAGE_SIZE), lambda b, pt: (b, 0, 0)),
                      pl.BlockSpec(memory_space=pl.ANY)],
            out_specs=pl.BlockSpec((None, width2 // 2, 1), lambda b, pt: (b, 0, 0)),
            scratch_shapes=[pltpu.VMEM((2, width2, PAGE_SIZE), F32), pltpu.SemaphoreType.DMA((2,))]),
        out_shape=jax.ShapeDtypeStruct((bsz, width2 // 2, 1), F32),
        compiler_params=_cparams(("arbitrary",)), name="sb_sample",
    )(page_table, q_lanes, c_sb_t)


def _moba_select_kernel(pt_ref, qm_ref, *refs, n_pages):
    page_refs = refs[:PAGES_PER_STEP]
    idx_ref, means_ref = refs[PAGES_PER_STEP:]
    s = pl.program_id(1)
    pages_per_block = MOBA_BLOCK // PAGE_SIZE
    n_blk = n_pages // pages_per_block
    blane = _iota((1, 128), 1)

    @pl.when(s == 0)
    def _():
        means_ref[...] = jnp.zeros_like(means_ref)

    means = means_ref[...]
    ones = jnp.ones((PAGE_SIZE, 128), BF16)
    keys = []
    for u, page_ref in enumerate(page_refs):
        keys.append(page_ref[...])
        if len(keys) == pages_per_block:
            ksum = sum(keys[1:], keys[0])
            hi = ksum.astype(BF16)
            lo = (ksum - hi.astype(F32)).astype(BF16)
            mean = (_dot(hi, ones) + _dot(lo, ones)) * (1.0 / MOBA_BLOCK)
            means = jnp.where(blane == (s * PAGES_PER_STEP + u) // pages_per_block, mean, means)
            keys = []
    means_ref[...] = means

    @pl.when(s == pl.num_programs(1) - 1)
    def _():
        gsc = jnp.where(blane < n_blk, _dot(qm_ref[...], means, HI), NEG)
        blk = blane.astype(F32)
        vec = jnp.zeros((MOBA_HEADS, 128), F32)
        for r in range(MOBA_TOPK):
            best = jnp.max(gsc, axis=1, keepdims=True)
            first = jnp.min(jnp.where(gsc == best, blk, float(BIG_IDX)), axis=1, keepdims=True)
            vec = jnp.where(blane == r, first, vec)
            gsc = jnp.where(blk == first, NEG, gsc)
        idx_ref[...] = vec.astype(jnp.int32)


def moba_sample_select(page_table, qm, c_moba_t):
    bsz, n_pages = page_table.shape
    width = c_moba_t.shape[1] // 2
    assert n_pages % PAGES_PER_STEP == 0 and MOBA_TOPK <= n_pages * PAGE_SIZE // MOBA_BLOCK <= 128

    def page_spec(u):
        return pl.BlockSpec((None, width, PAGE_SIZE), lambda b, s, pt: (pt[b, s * PAGES_PER_STEP + u], 0, 0))

    return pl.pallas_call(
        functools.partial(_moba_select_kernel, n_pages=n_pages),
        grid_spec=pltpu.PrefetchScalarGridSpec(
            num_scalar_prefetch=1,
            grid=(bsz, n_pages // PAGES_PER_STEP),
            in_specs=[pl.BlockSpec((None, MOBA_HEADS, width), lambda b, s, pt: (b, 0, 0))]
                     + [page_spec(u) for u in range(PAGES_PER_STEP)],
            out_specs=pl.BlockSpec((None, MOBA_HEADS, 128), lambda b, s, pt: (b, 0, 0)),
            scratch_shapes=[pltpu.VMEM((width, 128), F32)]),
        out_shape=jax.ShapeDtypeStruct((bsz, MOBA_HEADS, 128), jnp.int32),
        compiler_params=_cparams(("parallel", "arbitrary")), name="moba_sample_select",
    )(page_table, qm, *([c_moba_t] * PAGES_PER_STEP))


def _moba_attend_kernel(pt_ref, idx_ref, q_ref, knew_ref, vnew_ref, slope_ref, cache_ref, o_ref, kbuf_ref, vbuf_ref,
                        sem, *, n_pages):
    b = pl.program_id(0)
    pages_per_block = MOBA_BLOCK // PAGE_SIZE
    n_sel = MOBA_TOPK * pages_per_block
    width = MOBA_KV * HEAD_DIM

    def page_copies(h, u):
        g = h // MOBA_GROUP
        page = pt_ref[b, idx_ref[b, h * MOBA_TOPK + u // pages_per_block] * pages_per_block + u % pages_per_block]
        dst = pl.ds(u * PAGE_SIZE, PAGE_SIZE)
        return (pltpu.make_async_copy(cache_ref.at[page, pl.ds(g * HEAD_DIM, HEAD_DIM), :],
                                      kbuf_ref.at[h, :, dst], sem),
                pltpu.make_async_copy(cache_ref.at[page, pl.ds(width + g * HEAD_DIM, HEAD_DIM), :],
                                      vbuf_ref.at[h, :, dst], sem))

    for h in range(MOBA_HEADS):
        for u in range(n_sel):
            for cp in page_copies(h, u):
                cp.start()
    for h in range(MOBA_HEADS):
        for u in range(n_sel):
            for cp in page_copies(h, u):
                cp.wait()

    n_keys = n_sel * PAGE_SIZE
    shape3 = (MOBA_HEADS, HEAD_DIM, PAGE_SIZE)
    q3 = q_ref[...].reshape(shape3)
    z = jnp.sum(kbuf_ref[...] * jnp.concatenate([q3] * n_sel, axis=2), axis=1)
    lane = _iota((1, n_keys), 1)
    hrow = _iota((MOBA_HEADS, 1), 0)
    kpos = jnp.zeros((MOBA_HEADS, n_keys), jnp.int32)
    for h in range(MOBA_HEADS):
        row = lane % MOBA_BLOCK
        for r in range(MOBA_TOPK):
            row = row + jnp.where(lane // MOBA_BLOCK == r, idx_ref[b, h * MOBA_TOPK + r] * MOBA_BLOCK, 0)
        kpos = jnp.where(hrow == h, row, kpos)
    qpos = n_pages * PAGE_SIZE
    sc = z - slope_ref[:, 0:1] * (qpos - kpos).astype(F32)
    knew = knew_ref[...].reshape(MOBA_HEADS, HEAD_DIM, 1)
    s_new = jnp.sum(q3[:, :, 0:1] * knew, axis=1)
    mx = jnp.maximum(jnp.max(sc, axis=1, keepdims=True), s_new)
    e = jnp.exp(sc - mx)
    e_new = jnp.exp(s_new - mx)
    den = jnp.sum(e, axis=1, keepdims=True) + e_new
    pv = vbuf_ref[...] * e[:, None, :]
    tiles = pv[:, :, 0:PAGE_SIZE]
    for u in range(1, n_sel):
        tiles = tiles + pv[:, :, u * PAGE_SIZE:(u + 1) * PAGE_SIZE]
    o = jnp.sum(tiles, axis=2, keepdims=True) + e_new[:, :, None] * vnew_ref[...].reshape(MOBA_HEADS, HEAD_DIM, 1)
    o_ref[...] = (o / den[:, :, None]).reshape(MOBA_HEADS * HEAD_DIM, 1)


def moba_sample_attend(page_table, idx, q_lanes, knew, vnew, c_moba_t):
    bsz, n_pages = page_table.shape
    hd = MOBA_HEADS * HEAD_DIM
    n_keys = MOBA_TOPK * MOBA_BLOCK
    per_b = lambda *shape: pl.BlockSpec((None,) + shape, lambda b, pt, ix: (b, 0, 0))
    return pl.pallas_call(
        functools.partial(_moba_attend_kernel, n_pages=n_pages),
        grid_spec=pltpu.PrefetchScalarGridSpec(
            num_scalar_prefetch=2,
            grid=(bsz,),
            in_specs=[per_b(hd, PAGE_SIZE), per_b(hd, 1), per_b(hd, 1),
                      pl.BlockSpec((MOBA_HEADS, 128), lambda b, pt, ix: (0, 0)),
                      pl.BlockSpec(memory_space=pl.ANY)],
            out_specs=per_b(hd, 1),
            scratch_shapes=[pltpu.VMEM((MOBA_HEADS, HEAD_DIM, n_keys), F32),
                            pltpu.VMEM((MOBA_HEADS, HEAD_DIM, n_keys), F32),
                            pltpu.SemaphoreType.DMA(())]),
        out_shape=jax.ShapeDtypeStruct((bsz, hd, 1), F32),
        compiler_params=_cparams(("arbitrary",)), name="moba_sample_attend",
    )(page_table, idx, q_lanes, knew, vnew, _head_slopes(MOBA_HEADS), c_moba_t)


def _block_diag_query(q, n_groups):
    bsz, n_heads, d = q.shape
    onehot = (jnp.arange(n_heads)[:, None] // (n_heads // n_groups) == jnp.arange(n_groups)[None, :]).astype(q.dtype)
    return (q[:, :, None, :] * onehot[None, :, :, None]).reshape(bsz, n_heads, n_groups * d)


def _pages_feature_major(cache):
    n_phys, page = cache.shape[:2]
    return jnp.transpose(cache, (0, 2, 3, 4, 1)).reshape(n_phys, -1, page)


def _ab_sample_weights(w):
    g0, g1 = AB_COLS["gate"]
    w_std = jnp.concatenate([w[:, :g0], jnp.pad(w[:, g0:g1], ((0, 0), (0, 128 - N_GATE))), w[:, g1:]],
                            axis=1).astype(BF16)
    w_t = w.T[AB_COLS["win"][0]:AB_COLS["win"][1]].astype(BF16)
    return w_std, w_t


def ab_sample_mix(y, g_mix, w_std, w_t, c_slc, c_cmp, c_win, c_sb, page_table):
    bsz = y.shape[0]
    n_phys = c_slc.shape[0]
    std = []
    off = 0
    for wd in AB_WIDTHS:
        std.append((off, wd, F32))
        off += wd
    width = 2 * NSA_KV * HEAD_DIM
    q_a, kv_cmp, kv_slc, kv_win, gate, q_b, kv_b, kvw_t = rms_proj(y, g_mix, w_std, std, w_t, ((0, width, F32),),
                                                                   1, bsz, bsz)
    q_a = q_a.reshape(bsz, NSA_HEADS, HEAD_DIM)
    qm_kv = jnp.pad(_block_diag_query(q_a, NSA_KV), ((0, 0), (0, 0), (0, NSA_KV * HEAD_DIM)))
    zeros = jnp.zeros_like(qm_kv)
    qm_cmp = jnp.concatenate([jnp.concatenate([qm_kv, zeros], -1), jnp.concatenate([zeros, qm_kv], -1)], axis=1)
    o_cmp, idx = nsa_sample_select(page_table, qm_cmp, c_cmp.reshape(n_phys, -1))
    gate3 = jnp.pad(gate[:, :N_GATE].reshape(bsz, NSA_HEADS, 3), ((0, 0), (0, 0), (0, 125)))
    c_win_t = jnp.transpose(c_win, (0, 2, 3, 4, 1)).reshape(bsz, width, -1)
    o_a, s_win_t = nsa_sample_attend(page_table, idx[:, :NSA_KV, :N_SEL_OLD].reshape(bsz, -1), q_a, gate3, o_cmp,
                                     kv_slc.reshape(bsz, 1, width), kv_win.reshape(bsz, 1, width), kvw_t,
                                     c_win_t, _pages_feature_major(c_slc))
    q_lanes = jnp.broadcast_to((q_b * SCALE)[:, :, None], (bsz, SB_HEADS * HEAD_DIM, PAGE_SIZE))
    o_b = sb_sample(page_table, q_lanes, _pages_feature_major(c_sb))
    o_ts = jnp.concatenate([o_a.reshape(bsz, -1), o_b.reshape(bsz, -1)], axis=1).T[None].astype(BF16)
    shape5 = (bsz, 1, 2, NSA_KV, HEAD_DIM)
    s_win = jnp.transpose(s_win_t.reshape(bsz, 2, NSA_KV, HEAD_DIM, -1), (0, 4, 1, 2, 3))
    return (o_ts, kv_slc.reshape(shape5), kv_cmp.reshape(shape5), s_win,
            kv_b.reshape(bsz, 1, 2, SB_HEADS, HEAD_DIM))


def c_sample_mix(y, g_mix, w_std, c_moba, page_table):
    bsz = y.shape[0]
    q, kv = rms_proj(y, g_mix, w_std, ((0, 1024, F32), (1024, 512, F32)), None, (), 1, bsz, bsz)
    qm = _block_diag_query(q.reshape(bsz, MOBA_HEADS, HEAD_DIM), MOBA_KV)
    pages = _pages_feature_major(c_moba)
    idx = moba_sample_select(page_table, qm, pages)[:, :, :MOBA_TOPK].reshape(bsz, -1)
    hd = MOBA_HEADS * HEAD_DIM
    q_lanes = jnp.broadcast_to((q * SCALE)[:, :, None], (bsz, hd, PAGE_SIZE))
    per_head = lambda x: jnp.broadcast_to(x.reshape(bsz, MOBA_KV, 1, HEAD_DIM),
                                          (bsz, MOBA_KV, MOBA_GROUP, HEAD_DIM)).reshape(bsz, hd, 1)
    kw = MOBA_KV * HEAD_DIM
    o = moba_sample_attend(page_table, idx, q_lanes, per_head(kv[:, :kw]), per_head(kv[:, kw:]), pages)
    return o.reshape(bsz, hd).T[None].astype(BF16), kv.reshape(bsz, 1, 2, MOBA_KV, HEAD_DIM)


def kernel(x_prompt, x_sample, cache_nsa_slc, cache_nsa_cmp, cache_nsa_win, cache_sb, cache_moba, page_table,
           norm_mix, w_in_ab, w_out_ab, cmp_pos, cmp_w1, cmp_w2, w_in_c, w_out_c,
           norm_ffn, w_up, w_down, norm_final):
    bsz, seq, d = x_prompt.shape
    tm = 512
    w_up_b = w_up.astype(BF16)
    w_down_b = w_down.astype(BF16)
    dbsz = x_sample.shape[0]
    yp = x_prompt.reshape(bsz * seq, d)
    ys = x_sample.reshape(dbsz, d)
    w_oab = w_out_ab[0].astype(BF16)
    w_oc = w_out_c[0].astype(BF16)
    wc = w_in_c[0]
    k_lo, k_hi = MOBA_HEADS * HEAD_DIM, (MOBA_HEADS + MOBA_KV) * HEAD_DIM

    ab_std, ab_t = _ab_prompt_weights(w_in_ab[0])
    o_ts, p_slc, p_cmp, p_win, p_sb = ab_prompt_mix(yp, norm_mix[0], ab_std, ab_t, cmp_pos[0], cmp_w1[0], cmp_w2[0],
                                                    bsz, seq, tm)
    yp = mix_mlp(yp, o_ts, w_oab, norm_ffn[0], w_up_b[0], w_down_b[0], norm_final, False, bsz, seq, tm)
    o_ts, p_moba = c_prompt_mix(yp, norm_mix[1], wc[:, k_lo:k_hi].astype(BF16), wc.T.astype(BF16), bsz, seq, tm)
    yp = mix_mlp(yp, o_ts, w_oc, norm_ffn[1], w_up_b[1], w_down_b[1], norm_final, True, bsz, seq, tm)

    abs_std, abs_t = _ab_sample_weights(w_in_ab[0])
    o_ts, s_slc, s_cmp, s_win, s_sb = ab_sample_mix(ys, norm_mix[0], abs_std, abs_t, cache_nsa_slc[0],
                                                    cache_nsa_cmp[0], cache_nsa_win[0], cache_sb[0], page_table)
    ys = mix_mlp(ys, o_ts, w_oab, norm_ffn[0], w_up_b[0], w_down_b[0], norm_final, False, 1, dbsz, dbsz)
    o_ts, s_moba = c_sample_mix(ys, norm_mix[1], wc.astype(BF16), cache_moba[0], page_table)
    ys = mix_mlp(ys, o_ts, w_oc, norm_ffn[1], w_up_b[1], w_down_b[1], norm_final, True, 1, dbsz, dbsz)

    return (yp.reshape(bsz, seq, d), ys.reshape(dbsz, 1, d), p_slc[None], p_cmp[None], p_win[None], p_sb[None],
            p_moba[None], s_slc[None], s_cmp[None], s_win[None], s_sb[None], s_moba[None])
```

```python
import functools

import jax
import jax.numpy as jnp
from jax import lax
from jax.experimental import pallas as pl
from jax.experimental.pallas import tpu as pltpu

HEAD_DIM = 64
NSA_HEADS = 8
NSA_KV = 2
NSA_GROUP = NSA_HEADS // NSA_KV
CMP_BLOCK = 64
SEL_TOPN = 16
WINDOW = 512
SB_HEADS = 8
MOBA_HEADS = 16
MOBA_KV = 4
MOBA_GROUP = MOBA_HEADS // MOBA_KV
MOBA_BLOCK = 256
MOBA_TOPK = 3
PAGE_SIZE = 128
EPS = 1e-6
NEG = -1e30
MASK_BIAS = -(2.0 ** 100)
FORCE = 1e4
SCALE = HEAD_DIM ** -0.5

F32 = jnp.float32
BF16 = jnp.bfloat16
HI = lax.Precision.HIGHEST

VMEM_LIMIT = 48 * 1024 * 1024

NSA_TQ = 128
NSA_TK = 256
SB_T = 256
SB_CUTOFF = -110.0
EXP_CUTOFF = -110.0
NORM_SLACK = 1.02
AUG = 16
N_GATE = 3 * NSA_HEADS


def _cparams(sem):
    return pltpu.CompilerParams(dimension_semantics=sem, vmem_limit_bytes=VMEM_LIMIT)


def _alibi_slopes(n):
    return [2.0 ** (-8.0 * (h + 1) / n) for h in range(n)]


def _nt(a, b, precision=None):
    return lax.dot_general(a, b, (((1,), (1,)), ((), ())), precision=precision, preferred_element_type=F32)


def _tn(a, b):
    return lax.dot_general(a, b, (((0,), (0,)), ((), ())), preferred_element_type=F32)


def _dot(a, b, precision=None):
    return jnp.dot(a, b, precision=precision, preferred_element_type=F32)


def _rms(x, g):
    return x * lax.rsqrt(jnp.mean(x * x, axis=-1, keepdims=True) + EPS) * g


def _softplus(z):
    return jnp.maximum(z, 0.0) + jnp.log(1.0 + jnp.exp(-jnp.abs(z)))


def _iota(shape, axis):
    return lax.broadcasted_iota(jnp.int32, shape, axis)


def _proj_kernel(*refs, std, tr):
    x_ref, g_ref = refs[:2]
    refs = refs[2:]
    ws_ref = wt_ref = None
    if std:
        ws_ref, refs = refs[0], refs[1:]
    if tr:
        wt_ref, refs = refs[0], refs[1:]
    h = _rms(x_ref[...], g_ref[...]).astype(BF16)
    k = 0
    for off, wd, _ in std:
        refs[k][...] = _dot(h, ws_ref[:, off:off + wd]).astype(refs[k].dtype)
        k += 1
    for off, rows, _ in tr:
        refs[k][...] = _nt(wt_ref[off:off + rows, :], h).astype(refs[k].dtype)
        k += 1


def rms_proj(x, g, w_std, std, w_t, tr, bsz, seq, tm):
    m, d = x.shape
    nj = seq // tm
    assert m == bsz * seq and seq % tm == 0
    const = lambda i: (0, 0)
    operands = [x, g.reshape(1, d)]
    in_specs = [pl.BlockSpec((tm, d), lambda i: (i, 0)), pl.BlockSpec((1, d), const)]
    if std:
        operands.append(w_std)
        in_specs.append(pl.BlockSpec(w_std.shape, const))
    if tr:
        operands.append(w_t)
        in_specs.append(pl.BlockSpec(w_t.shape, const))
    out_specs = ([pl.BlockSpec((tm, wd), lambda i: (i, 0)) for _, wd, _ in std]
                 + [pl.BlockSpec((None, rows, tm), lambda i: (i // nj, 0, i % nj)) for _, rows, _ in tr])
    out_shape = ([jax.ShapeDtypeStruct((m, wd), dt) for _, wd, dt in std]
                 + [jax.ShapeDtypeStruct((bsz, rows, seq), dt) for _, rows, dt in tr])
    return pl.pallas_call(
        functools.partial(_proj_kernel, std=tuple(std), tr=tuple(tr)),
        grid=(m // tm,), in_specs=in_specs, out_specs=out_specs, out_shape=out_shape,
        compiler_params=_cparams(("parallel",)), name="rms_proj",
    )(*operands)


def _mix_mlp_kernel(y_ref, o_ref, wo_ref, g_ref, wu_ref, wd_ref, gf_ref, out_ref, *, ff_chunk, final):
    y = y_ref[...] + _tn(o_ref[...], wo_ref[...])
    h = _rms(y, g_ref[...]).astype(BF16)
    acc = y
    for c in range(0, wu_ref.shape[1], ff_chunk):
        a = jnp.maximum(_dot(h, wu_ref[:, c:c + ff_chunk]), 0.0)
        acc = acc + _dot((a * a).astype(BF16), wd_ref[c:c + ff_chunk, :])
    if final:
        acc = _rms(acc, gf_ref[...])
    out_ref[...] = acc


def mix_mlp(y, o_t, w_out, g_ffn, w_up, w_down, g_final, final, bsz, seq, tm):
    m, d = y.shape
    f = w_up.shape[1]
    nj = seq // tm
    const = lambda i: (0, 0)
    return pl.pallas_call(
        functools.partial(_mix_mlp_kernel, ff_chunk=min(f, 1024), final=final),
        grid=(m // tm,),
        in_specs=[pl.BlockSpec((tm, d), lambda i: (i, 0)),
                  pl.BlockSpec((None, o_t.shape[1], tm), lambda i: (i // nj, 0, i % nj)),
                  pl.BlockSpec((d, d), const), pl.BlockSpec((1, d), const), pl.BlockSpec((d, f), const),
                  pl.BlockSpec((f, d), const), pl.BlockSpec((1, d), const)],
        out_specs=pl.BlockSpec((tm, d), lambda i: (i, 0)),
        out_shape=jax.ShapeDtypeStruct((m, d), F32),
        compiler_params=_cparams(("parallel",)), name="mix_mlp",
    )(y, o_t, w_out, g_ffn.reshape(1, d), w_up, w_down, g_final.reshape(1, d))


def _compress_kernel(x_ref, pos_ref, w1_ref, w2_ref, o_ref):
    x = (x_ref[...] + pos_ref[...]).astype(BF16)
    hid = _dot(x, w1_ref[...])
    hid = hid * (1.0 / (1.0 + jnp.exp(-hid)))
    o_ref[...] = _dot(hid.astype(BF16), w2_ref[...])


def compress_blocks(x, pos, w1_bf16, w2_bf16, tr):
    _, r, fdim = x.shape
    hdim = w1_bf16.shape[2]
    return pl.pallas_call(
        _compress_kernel,
        grid=(2, r // tr),
        in_specs=[pl.BlockSpec((None, tr, fdim), lambda k, i: (k, i, 0)),
                  pl.BlockSpec((None, 1, fdim), lambda k, i: (k, 0, 0)),
                  pl.BlockSpec((None, fdim, hdim), lambda k, i: (k, 0, 0)),
                  pl.BlockSpec((None, hdim, HEAD_DIM), lambda k, i: (k, 0, 0))],
        out_specs=pl.BlockSpec((None, tr, HEAD_DIM), lambda k, i: (k, i, 0)),
        out_shape=jax.ShapeDtypeStruct((2, r, HEAD_DIM), F32),
        compiler_params=_cparams(("parallel", "parallel")), name="nsa_compress",
    )(x, pos, w1_bf16, w2_bf16)


def _biased_update(carry, s, shift, vt):
    m, l, acc = carry
    m_new = jnp.maximum(m, jnp.max(s, axis=0, keepdims=True) + shift)
    alpha = jnp.exp(m - m_new)
    p = jnp.exp(s - (m_new - shift))
    l = alpha * l + jnp.sum(p, axis=0, keepdims=True)
    acc = alpha * acc + _dot(vt, p.astype(BF16))
    return m_new, l, acc


def _slope_rows(slope, lanes):
    p1 = slope.astype(BF16).astype(F32)
    p2 = (slope - p1).astype(BF16).astype(F32)
    p3 = slope - p1 - p2
    return jnp.concatenate([p1, p2, p3, jnp.zeros((5, lanes), F32)], axis=0)


def _softmax_init(lanes):
    return (jnp.full((1, lanes), NEG, F32), jnp.zeros((1, lanes), F32), jnp.zeros((HEAD_DIM, lanes), F32))


def _join(chains):
    return tuple(jnp.concatenate(parts, axis=1) for parts in zip(*chains))


def _split(joined, n):
    w = joined[0].shape[1] // n
    return [tuple(x[:, r * w:(r + 1) * w] for x in joined) for r in range(n)]


def _nsa_prompt_kernel(qt_ref, gate_ref, kc_ref, vct_ref, ks_ref, vst_ref, kw_ref, vwt_ref, slope_ref,
                       o_ref, selneg_ref, *, nb):
    g = pl.program_id(1)
    i = pl.program_id(2)
    tq, tk = NSA_TQ, NSA_TK
    lanes = NSA_GROUP * tq
    blocks_per_chunk = tk // CMP_BLOCK
    qt = jnp.concatenate([qt_ref[r * HEAD_DIM:(r + 1) * HEAD_DIM, :] for r in range(NSA_GROUP)], axis=1)
    qs = qt * SCALE
    rowgrp = _iota((NSA_KV * HEAD_DIM, 1), 0) // HEAD_DIM
    qpad = jnp.where(rowgrp == g, jnp.concatenate([qs] * NSA_KV, axis=0), 0.0).astype(BF16)
    slope = slope_ref[...]
    t_row = i * tq + _iota((1, lanes), 1) % tq
    tf = t_row.astype(F32)

    cend = _iota((nb, 1), 0) * CMP_BLOCK + (CMP_BLOCK - 1)
    sc = _dot(kc_ref[...], qs, HI) - slope * (tf - cend.astype(F32))
    vis = cend <= t_row
    sc = jnp.where(vis, sc, NEG)
    e = jnp.where(vis, jnp.exp(sc - jnp.max(sc, axis=0, keepdims=True)), 0.0)
    den = jnp.sum(e, axis=0, keepdims=True)
    p = e * jnp.where(den > 0.0, 1.0 / den, 0.0)
    o_cmp = _dot(vct_ref[...], p, HI)

    imp = p[:, 0:tq]
    for r in range(1, NSA_GROUP):
        imp = imp + p[:, r * tq:(r + 1) * tq]
    j = _iota((nb, tq), 0)
    own = (i * tq + _iota((nb, tq), 1)) // CMP_BLOCK
    forced = (j == 0) | (j == own) | (j == own - 1)
    score = jnp.where(forced, FORCE, imp)
    score = jnp.where(j <= own, score, -1.0)
    rank = jnp.zeros((nb, tq), F32)
    for m in range(nb):
        sm = score[m:m + 1, :]
        beats = (sm > score) | ((sm == score) & (j > m))
        rank = rank + jnp.where(beats, 1.0, 0.0)
    selneg = jnp.where((rank < float(min(SEL_TOPN, nb))) & (score >= 0.0), 0.0, MASK_BIAS)
    for c in range(nb // blocks_per_chunk):
        selneg_ref[c] = jnp.concatenate([selneg[c * blocks_per_chunk:(c + 1) * blocks_per_chunk],
                                         jnp.zeros((8 - blocks_per_chunk, tq), F32)], axis=0)

    sub = _iota((tk, 1), 0)
    col = _iota((tk, AUG), 1)
    aug = jnp.where(col < blocks_per_chunk, jnp.where(sub // CMP_BLOCK == col, 1.0, 0.0),
                    jnp.where((col >= 8) & (col < 11), sub.astype(F32), 0.0)).astype(BF16)
    slope8 = _slope_rows(slope, lanes)
    cd = (i * tq) // tk
    rel = (i * tq - cd * tk) + _iota((1, lanes), 1) % tq
    zeros8 = jnp.zeros((8, lanes), F32)
    causal = jnp.where(sub <= rel, 0.0, MASK_BIAS)
    win_tail = jnp.where((sub > rel) & (cd >= 2), 0.0, MASK_BIAS)

    def scores(k_ref, c, mask_rows):
        ks = pl.multiple_of(c * tk, tk)
        bias_rows = jnp.concatenate([mask_rows, slope8], axis=0).astype(BF16)
        return _dot(k_ref[pl.ds(ks, tk), :], qpad) + _dot(aug, bias_rows)

    def update(carry, s, vt_ref, c):
        ks = pl.multiple_of(c * tk, tk)
        return _biased_update(carry, s, slope * ((c * tk).astype(F32) - tf), vt_ref[:, pl.ds(ks, tk)].astype(BF16))

    def sel_rows(c):
        return jnp.concatenate([selneg_ref[c]] * NSA_GROUP, axis=1)

    init = _softmax_init(lanes)
    carry = update(init, scores(ks_ref, cd, sel_rows(cd)) + causal, vst_ref, cd)

    def slc_body(c, state):
        carry, s = state
        nxt = jnp.minimum(c + 1, jnp.maximum(cd - 1, 0))
        s_next = scores(ks_ref, nxt, sel_rows(nxt))
        return update(carry, s, vst_ref, c), s_next

    (_, l_s, acc_s), _ = lax.fori_loop(0, cd, slc_body, (carry, scores(ks_ref, 0, sel_rows(0))))

    c1, c2 = jnp.maximum(cd - 1, 0), jnp.maximum(cd - 2, 0)
    s0 = scores(kw_ref, cd, zeros8) + causal
    s1 = scores(kw_ref, c1, jnp.where(cd >= 1, zeros8, MASK_BIAS))
    s2 = scores(kw_ref, c2, zeros8) + win_tail
    _, l_w, acc_w = update(update(update(init, s0, vwt_ref, cd), s1, vwt_ref, c1), s2, vwt_ref, c2)

    gates = []
    for c in range(3):
        row = jnp.concatenate([gate_ref[c * NSA_GROUP + r:c * NSA_GROUP + r + 1, :] for r in range(NSA_GROUP)],
                              axis=1)
        gates.append(1.0 / (1.0 + jnp.exp(-row)))
    o = gates[0] * o_cmp + gates[1] * (acc_s / l_s) + gates[2] * (acc_w / l_w)
    for r in range(NSA_GROUP):
        o_ref[r * HEAD_DIM:(r + 1) * HEAD_DIM, :] = o[:, r * tq:(r + 1) * tq].astype(o_ref.dtype)


def nsa_prompt(qa_t, gate_t, kc, vct, ks, kvs_t, kw, kvw_t, slopes):
    bsz, _, seq = qa_t.shape
    nb = kc.shape[2]
    tq = NSA_TQ
    lanes = NSA_GROUP * tq
    gw = NSA_GROUP * HEAD_DIM
    assert seq % NSA_TK == 0 and WINDOW == 2 * NSA_TK and NSA_TK % NSA_TQ == 0
    return pl.pallas_call(
        functools.partial(_nsa_prompt_kernel, nb=nb),
        grid=(bsz, NSA_KV, seq // tq),
        in_specs=[pl.BlockSpec((None, gw, tq), lambda b, g, i: (b, g, i)),
                  pl.BlockSpec((None, 16, tq), lambda b, g, i: (b, g, i)),
                  pl.BlockSpec((None, None, nb, HEAD_DIM), lambda b, g, i: (b, g, 0, 0)),
                  pl.BlockSpec((None, None, HEAD_DIM, nb), lambda b, g, i: (b, g, 0, 0)),
                  pl.BlockSpec((None, seq, NSA_KV * HEAD_DIM), lambda b, g, i: (b, 0, 0)),
                  pl.BlockSpec((None, HEAD_DIM, seq), lambda b, g, i: (b, NSA_KV + g, 0)),
                  pl.BlockSpec((None, seq, NSA_KV * HEAD_DIM), lambda b, g, i: (b, 0, 0)),
                  pl.BlockSpec((None, HEAD_DIM, seq), lambda b, g, i: (b, NSA_KV + g, 0)),
                  pl.BlockSpec((None, 1, lanes), lambda b, g, i: (g, 0, 0))],
        out_specs=pl.BlockSpec((None, gw, tq), lambda b, g, i: (b, g, i)),
        out_shape=jax.ShapeDtypeStruct((bsz, NSA_HEADS * HEAD_DIM, seq), BF16),
        scratch_shapes=[pltpu.VMEM((nb * CMP_BLOCK // NSA_TK, 8, tq), F32)],
        compiler_params=_cparams(("parallel", "parallel", "arbitrary")), name="nsa_prompt",
    )(qa_t, gate_t, kc, vct, ks, kvs_t, kw, kvw_t, slopes)


def _sb_prompt_kernel(q_ref, k_ref, vt_ref, o_ref):
    i = pl.program_id(2)
    t = SB_T
    qpair = (q_ref[...].astype(F32) * SCALE).astype(BF16)
    rowh = _iota((2 * HEAD_DIM, 1), 0) // HEAD_DIM
    qcat = jnp.concatenate([jnp.where(rowh == hh, qpair, jnp.zeros_like(qpair)) for hh in range(2)], axis=1)
    th = t // 2
    after = jnp.where(_iota((th, th), 0) < _iota((th, th), 1), 1.0, 0.0).astype(BF16)
    vis = _iota((t, 1), 0) < _iota((1, 2 * t), 1) % t

    def logits(kb):
        return _dot(k_ref[pl.ds(pl.multiple_of(kb * t, t), t), :], qcat)

    def step(kb, carry, z, diagonal):
        acc, run = carry
        ks = pl.multiple_of(kb * t, t)
        sp = _softplus(z)
        ls = -sp
        if diagonal:
            ls = jnp.where(vis, ls, 0.0)
        hi = ls.astype(BF16)
        lo = (ls - hi.astype(F32)).astype(BF16)
        later = []
        for half in (1, 0):
            rows = slice(half * th, (half + 1) * th)
            later.insert(0, _dot(after, hi[rows]) + _dot(after, lo[rows]) + run)
            run = run + jnp.sum(ls[rows], axis=0, keepdims=True)
        w = jnp.exp(z - sp + jnp.concatenate(later, axis=0))
        if diagonal:
            w = jnp.where(vis, w, 0.0)
        w = w.astype(BF16)
        pv = [_dot(vt_ref[hh * HEAD_DIM:(hh + 1) * HEAD_DIM, pl.ds(ks, t)].astype(BF16), w[:, hh * t:(hh + 1) * t])
              for hh in range(2)]
        return acc + jnp.concatenate(pv, axis=1), run

    def body(state):
        st, carry, z = state
        kb = i - st
        z_next = logits(jnp.maximum(kb - 1, 0))
        return st + 1, step(kb, carry, z, False), z_next

    def more(state):
        st, (_, run), _ = state
        return (st <= i) & (jnp.max(run) > SB_CUTOFF)

    carry = step(i, (jnp.zeros((HEAD_DIM, 2 * t), F32), jnp.zeros((1, 2 * t), F32)), logits(i), True)
    _, (acc, _), _ = lax.while_loop(more, body, (jnp.int32(1), carry, logits(jnp.maximum(i - 1, 0))))
    for hh in range(2):
        o_ref[hh * HEAD_DIM:(hh + 1) * HEAD_DIM, :] = acc[:, hh * t:(hh + 1) * t].astype(o_ref.dtype)


def sb_prompt(qb_t, kb, kvb_t):
    bsz, width, seq = qb_t.shape
    pw = 2 * HEAD_DIM
    n_pairs = width // pw
    return pl.pallas_call(
        _sb_prompt_kernel,
        grid=(bsz, n_pairs, seq // SB_T),
        in_specs=[pl.BlockSpec((None, pw, SB_T), lambda b, p, i: (b, p, i)),
                  pl.BlockSpec((None, seq, pw), lambda b, p, i: (b, 0, p)),
                  pl.BlockSpec((None, pw, seq), lambda b, p, i: (b, n_pairs + p, 0))],
        out_specs=pl.BlockSpec((None, pw, SB_T), lambda b, p, i: (b, p, i)),
        out_shape=jax.ShapeDtypeStruct((bsz, width, seq), BF16),
        compiler_params=_cparams(("parallel", "parallel", "arbitrary")), name="sb_prompt",
    )(qb_t, kb, kvb_t)


def _block_mean_kernel(k_ref, o_ref, n2_ref):
    k = k_ref[...]
    o_ref[...] = jnp.sum(k, axis=0, keepdims=True) * (1.0 / MOBA_BLOCK)
    sq = k * k
    rows = [jnp.broadcast_to(jnp.max(jnp.sum(sq[:, g * HEAD_DIM:(g + 1) * HEAD_DIM], axis=1, keepdims=True),
                                     axis=0, keepdims=True), (1, HEAD_DIM)) for g in range(MOBA_KV)]
    n2_ref[...] = jnp.concatenate(rows, axis=1)


def moba_block_means(k, bsz, seq):
    nblk = seq // MOBA_BLOCK
    width = k.shape[1]
    out = pl.BlockSpec((None, None, 1, width), lambda b, n: (b, n, 0, 0))
    return pl.pallas_call(
        _block_mean_kernel,
        grid=(bsz, nblk),
        in_specs=[pl.BlockSpec((MOBA_BLOCK, width), lambda b, n: (b * nblk + n, 0))],
        out_specs=[out, out],
        out_shape=[jax.ShapeDtypeStruct((bsz, nblk, 1, width), F32)] * 2,
        compiler_params=_cparams(("parallel", "parallel")), name="moba_block_means",
    )(k)


def _moba_prompt_kernel(qt_ref, means_ref, kmax_ref, k_ref, vt_ref, slope_ref, o_ref, selneg_ref, *, nblk):
    g = pl.program_id(1)
    i = pl.program_id(2)
    tq = MOBA_BLOCK
    lanes = MOBA_GROUP * tq
    qt = jnp.concatenate([qt_ref[r * HEAD_DIM:(r + 1) * HEAD_DIM, :] for r in range(MOBA_GROUP)], axis=1)
    qs = qt * SCALE
    rowgrp = _iota((MOBA_KV * HEAD_DIM, 1), 0) // HEAD_DIM
    qpad = jnp.where(rowgrp == g, jnp.concatenate([qs] * MOBA_KV, axis=0), 0.0).astype(BF16)
    slope = slope_ref[...]
    t_row = i * tq + _iota((1, lanes), 1) % tq
    tf = t_row.astype(F32)

    gsc = _dot(means_ref[...], qt, HI)
    n_idx = _iota((nblk, lanes), 0)
    past = n_idx < i
    gsc = jnp.where(past, gsc, NEG)
    rank = jnp.zeros((nblk, lanes), F32)
    for m in range(nblk):
        gm = gsc[m:m + 1, :]
        beats = (gm > gsc) | ((gm == gsc) & (n_idx > m))
        rank = rank + jnp.where(beats, 1.0, 0.0)
    selneg_ref[...] = jnp.where((rank < float(min(MOBA_TOPK, nblk))) & past, 0.0, MASK_BIAS)

    sub = _iota((tq, 1), 0)
    ramp = slope * sub.astype(F32)

    def scores(n):
        return _dot(k_ref[pl.ds(pl.multiple_of(n * tq, tq), tq), :], qpad) + ramp

    def update(carry, s, n, mask_row):
        shift = slope * ((n * tq).astype(F32) - tf) + mask_row
        return _biased_update(carry, s, shift, vt_ref[:, pl.ds(pl.multiple_of(n * tq, tq), tq)].astype(BF16))

    causal = jnp.where(sub <= _iota((1, lanes), 1) % tq, 0.0, MASK_BIAS)
    carry = update(_softmax_init(lanes), scores(i) + causal, i, 0.0)

    bound = jnp.sqrt(jnp.sum(qs * qs, axis=0, keepdims=True)) * kmax_ref[:, 0:1] * NORM_SLACK
    reach = (bound - carry[0] - EXP_CUTOFF) / slope
    need = jnp.clip(jnp.ceil((reach - 1.0) * (1.0 / tq)), 0.0, float(nblk))
    n_lo = jnp.maximum(i - jnp.max(need, axis=1, keepdims=True).astype(jnp.int32)[0, 0], 0)

    def body(n, state):
        carry, s = state
        s_next = scores(jnp.minimum(n + 1, jnp.maximum(i - 1, 0)))
        return update(carry, s, n, selneg_ref[pl.ds(n, 1), :]), s_next

    (_, l, acc), _ = lax.fori_loop(n_lo, i, body, (carry, scores(jnp.minimum(n_lo, jnp.maximum(i - 1, 0)))))
    o = acc / l
    for r in range(MOBA_GROUP):
        o_ref[r * HEAD_DIM:(r + 1) * HEAD_DIM, :] = o[:, r * tq:(r + 1) * tq].astype(o_ref.dtype)


def moba_prompt(q_t, means, kmax, k, kv_t, slopes):
    bsz, _, seq = q_t.shape
    nblk = seq // MOBA_BLOCK
    gw = MOBA_GROUP * HEAD_DIM
    lanes = MOBA_GROUP * MOBA_BLOCK
    return pl.pallas_call(
        functools.partial(_moba_prompt_kernel, nblk=nblk),
        grid=(bsz, MOBA_KV, nblk),
        in_specs=[pl.BlockSpec((None, gw, MOBA_BLOCK), lambda b, g, i: (b, g, i)),
                  pl.BlockSpec((None, None, nblk, HEAD_DIM), lambda b, g, i: (b, g, 0, 0)),
                  pl.BlockSpec((None, None, 1, 128), lambda b, g, i: (b, g, 0, 0)),
                  pl.BlockSpec((None, seq, MOBA_KV * HEAD_DIM), lambda b, g, i: (b, 0, 0)),
                  pl.BlockSpec((None, HEAD_DIM, seq), lambda b, g, i: (b, MOBA_KV + g, 0)),
                  pl.BlockSpec((None, 1, lanes), lambda b, g, i: (g, 0, 0))],
        out_specs=pl.BlockSpec((None, gw, MOBA_BLOCK), lambda b, g, i: (b, g, i)),
        out_shape=jax.ShapeDtypeStruct((bsz, MOBA_HEADS * HEAD_DIM, seq), BF16),
        scratch_shapes=[pltpu.VMEM((nblk, lanes), F32)],
        compiler_params=_cparams(("parallel", "parallel", "arbitrary")), name="moba_prompt",
    )(q_t, means, kmax, k, kv_t, slopes)


AB_COLS = dict(q_a=(0, 512), cmp=(512, 768), slc=(768, 1024), win=(1024, 1280), gate=(1280, 1304),
               q_b=(1304, 1816), kv_b=(1816, 2840))
AB_WIDTHS = (512, 256, 256, 256, 128, 512, 1024)


def _cols(w, name, lo=0, hi=None):
    a, b = AB_COLS[name]
    return w[:, a + lo:(b if hi is None else a + hi)]


def _ab_prompt_weights(w):
    w_std = jnp.concatenate([_cols(w, "cmp"), _cols(w, "slc", 0, 128), _cols(w, "win", 0, 128),
                             _cols(w, "kv_b", 0, 512)], axis=1).astype(BF16)
    wt = w.T
    g0 = AB_COLS["gate"][0]
    idx, keep = [], []
    for g in range(NSA_KV):
        for c in range(4):
            for r in range(NSA_GROUP):
                idx.append(g0 + (g * NSA_GROUP + r) * 3 + min(c, 2))
                keep.append(1.0 if c < 3 else 0.0)
    gate_rows = wt[jnp.asarray(idx)] * jnp.asarray(keep, F32)[:, None]
    rows = lambda name: wt[AB_COLS[name][0]:AB_COLS[name][1]]
    w_t = jnp.concatenate([rows("q_a"), rows("slc"), rows("win"), gate_rows, rows("q_b"), rows("kv_b")],
                          axis=0).astype(BF16)
    return w_std, w_t


AB_STD = ((0, 256, F32), (256, 128, BF16), (384, 128, BF16), (512, 512, BF16))
AB_TR = ((0, 512, F32), (512, 256, F32), (768, 256, F32), (1024, 32, F32), (1056, 512, BF16), (1568, 1024, F32))


def _group_slopes(n_heads, n_kv, tq):
    sl = jnp.asarray(_alibi_slopes(n_heads), F32).reshape(n_kv, n_heads // n_kv, 1)
    return jnp.broadcast_to(sl, (n_kv, n_heads // n_kv, tq)).reshape(n_kv, 1, (n_heads // n_kv) * tq)


def _cache_view(x_t, n_heads):
    bsz, _, seq = x_t.shape
    return jnp.transpose(x_t.reshape(bsz, 2, n_heads, HEAD_DIM, seq), (0, 4, 1, 2, 3))


def ab_prompt_mix(y, g_mix, w_std, w_t, c_pos, c_w1, c_w2, bsz, seq, tm):
    kv_cmp, ks, kw, kb, qa_t, kvs_t, kvw_t, gate_t, qb_t, kvb_t = rms_proj(
        y, g_mix, w_std, AB_STD, w_t, AB_TR, bsz, seq, tm)
    nb = seq // CMP_BLOCK

    x = jnp.transpose(kv_cmp.reshape(bsz, nb, CMP_BLOCK, 2, NSA_KV, HEAD_DIM), (3, 0, 1, 4, 2, 5))
    rows = bsz * nb * NSA_KV
    x = x.reshape(2, rows, CMP_BLOCK * HEAD_DIM)
    cmp = compress_blocks(x, c_pos.reshape(2, 1, CMP_BLOCK * HEAD_DIM), c_w1.astype(BF16), c_w2.astype(BF16),
                          min(rows, 256))
    cmp = cmp.reshape(2, bsz, nb, NSA_KV, HEAD_DIM)
    p_cmp = jnp.transpose(cmp, (1, 2, 0, 3, 4))
    kc = jnp.transpose(cmp[0], (0, 2, 1, 3))
    vct = jnp.transpose(cmp[1], (0, 2, 3, 1))

    kw_full = NSA_KV * HEAD_DIM
    oa_t = nsa_prompt(qa_t, gate_t, kc, vct, ks.reshape(bsz, seq, kw_full), kvs_t, kw.reshape(bsz, seq, kw_full),
                      kvw_t, _group_slopes(NSA_HEADS, NSA_KV, NSA_TQ))
    ob_t = sb_prompt(qb_t, kb.reshape(bsz, seq, SB_HEADS * HEAD_DIM), kvb_t)
    wlen = min(WINDOW, seq)
    return (jnp.concatenate([oa_t, ob_t], axis=1), _cache_view(kvs_t, NSA_KV), p_cmp,
            _cache_view(kvw_t[:, :, seq - wlen:], NSA_KV),
            _cache_view(kvb_t, SB_HEADS))


C_STD = ((0, 256, F32), (0, 256, BF16))
C_TR = ((0, 1024, F32), (1024, 512, F32))


def c_prompt_mix(y, g_mix, w_std, w_t, bsz, seq, tm):
    k32, k16, q_t, kv_t = rms_proj(y, g_mix, w_std, C_STD, w_t, C_TR, bsz, seq, tm)
    nblk = seq // MOBA_BLOCK
    means, norm2 = moba_block_means(k32, bsz, seq)
    means = jnp.transpose(means.reshape(bsz, nblk, MOBA_KV, HEAD_DIM), (0, 2, 1, 3))
    kmax = jnp.sqrt(jnp.max(norm2.reshape(bsz, nblk, MOBA_KV, HEAD_DIM)[..., 0], axis=1))
    kmax = jnp.broadcast_to(kmax[:, :, None, None], (bsz, MOBA_KV, 1, 128))
    o_t = moba_prompt(q_t, means, kmax, k16.reshape(bsz, seq, MOBA_KV * HEAD_DIM), kv_t,
                      _group_slopes(MOBA_HEADS, MOBA_KV, MOBA_BLOCK))
    return o_t, _cache_view(kv_t, MOBA_KV)


PAGES_PER_STEP = 32
N_SEL_OLD = SEL_TOPN - 1
BIG_IDX = 1 << 30


def _head_slopes(n_heads):
    return jnp.broadcast_to(jnp.asarray(_alibi_slopes(n_heads), F32).reshape(n_heads, 1), (n_heads, 128))


def _fold_heads(wide, n_groups, heads_per_group):
    n_heads = n_groups * heads_per_group
    hrow = _iota((n_heads, HEAD_DIM), 0) // heads_per_group
    out = jnp.zeros((n_heads, HEAD_DIM), F32)
    for g in range(n_groups):
        out = out + jnp.where(hrow == g, wide[:, g * HEAD_DIM:(g + 1) * HEAD_DIM], 0.0)
    return out


def _nsa_select_kernel(pt_ref, qm_ref, slope_ref, ccmp_ref, ocmp_ref, idx_ref, buf_ref, sem, *, n_pages):
    b = pl.program_id(0)
    row_copy = lambda p, page: pltpu.make_async_copy(ccmp_ref.at[pl.ds(page, 1), :], buf_ref.at[pl.ds(p, 1), :], sem)

    def issue(p, c):
        row_copy(p, pt_ref[b, p]).start()
        return c

    def drain(p, c):
        row_copy(p, 0).wait()
        return c

    lax.fori_loop(0, n_pages, issue, 0)
    lax.fori_loop(0, n_pages, drain, 0)

    buf = buf_ref[...]
    slope = slope_ref[:, 0:1]
    qpos = float(n_pages * PAGE_SIZE)
    page = _iota((1, n_pages), 1)
    st = _nt(qm_ref[...], buf, HI) * SCALE
    halves = []
    for blk in range(2):
        cend = (page * PAGE_SIZE + blk * CMP_BLOCK + (CMP_BLOCK - 1)).astype(F32)
        halves.append(st[blk * NSA_HEADS:(blk + 1) * NSA_HEADS] - slope * (qpos - cend))
    mx = jnp.maximum(jnp.max(halves[0], axis=1, keepdims=True), jnp.max(halves[1], axis=1, keepdims=True))
    e = [jnp.exp(hv - mx) for hv in halves]
    inv = 1.0 / (jnp.sum(e[0], axis=1, keepdims=True) + jnp.sum(e[1], axis=1, keepdims=True))
    p = [ev * inv for ev in e]
    v_lo = NSA_KV * HEAD_DIM
    ocmp_ref[...] = _dot(p[0], buf[:, v_lo:2 * v_lo], HI) + _dot(p[1], buf[:, 3 * v_lo:4 * v_lo], HI)

    hrow = _iota((NSA_HEADS, n_pages), 0) // NSA_GROUP
    jidx = (2 * _iota((2, n_pages), 1) + _iota((2, n_pages), 0)).astype(F32)
    lane = _iota((1, 128), 1)
    n_blk = 2 * n_pages
    rows = []
    for g in range(NSA_KV):
        imp = jnp.concatenate([jnp.sum(jnp.where(hrow == g, pv, 0.0), axis=0, keepdims=True) for pv in p], axis=0)
        cand = jnp.where((jidx == 0.0) | (jidx == float(n_blk - 1)), -1.0, imp)
        vec = jnp.where(lane == N_SEL_OLD - 1, float(n_blk - 1), 0.0)
        for r in range(N_SEL_OLD - 2):
            best = jnp.max(jnp.max(cand, axis=1, keepdims=True), axis=0, keepdims=True)
            hit = jnp.where(cand == best, jidx, float(BIG_IDX))
            pick = jnp.min(jnp.min(hit, axis=1, keepdims=True), axis=0, keepdims=True)
            vec = jnp.where(lane == r, pick, vec)
            cand = jnp.where(jidx == pick, -2.0, cand)
        rows.append(vec)
    idx_ref[...] = jnp.concatenate(rows + [jnp.zeros((8 - NSA_KV, 128), F32)], axis=0).astype(jnp.int32)


def nsa_sample_select(page_table, qm_cmp, c_cmp):
    bsz, n_pages = page_table.shape
    width = c_cmp.shape[1]
    assert 2 * n_pages - 2 >= N_SEL_OLD - 2
    return pl.pallas_call(
        functools.partial(_nsa_select_kernel, n_pages=n_pages),
        grid_spec=pltpu.PrefetchScalarGridSpec(
            num_scalar_prefetch=1,
            grid=(bsz,),
            in_specs=[pl.BlockSpec((None, 2 * NSA_HEADS, width), lambda b, pt: (b, 0, 0)),
                      pl.BlockSpec((NSA_HEADS, 128), lambda b, pt: (0, 0)),
                      pl.BlockSpec(memory_space=pl.ANY)],
            out_specs=[pl.BlockSpec((None, NSA_HEADS, 128), lambda b, pt: (b, 0, 0)),
                       pl.BlockSpec((None, 8, 128), lambda b, pt: (b, 0, 0))],
            scratch_shapes=[pltpu.VMEM((n_pages, width), F32), pltpu.SemaphoreType.DMA(())]),
        out_shape=[jax.ShapeDtypeStruct((bsz, NSA_HEADS, 128), F32),
                   jax.ShapeDtypeStruct((bsz, 8, 128), jnp.int32)],
        compiler_params=_cparams(("arbitrary",)), name="nsa_sample_select",
    )(page_table, qm_cmp, _head_slopes(NSA_HEADS), c_cmp)


def _nsa_attend_kernel(pt_ref, idx_ref, q_ref, slope_ref, gate_ref, ocmp_ref, new_slc_ref, new_win_ref,
                       new_win_t_ref, cwin_ref, cslc_ref, o_ref, swin_ref, kbuf_ref, vbuf_ref, sem, *, n_pages):
    b = pl.program_id(0)
    gw = NSA_KV * HEAD_DIM

    def page_copies(g, r):
        page = pt_ref[b, idx_ref[b, g * N_SEL_OLD + r] // 2]
        dst = pl.ds(r * PAGE_SIZE, PAGE_SIZE)
        return (pltpu.make_async_copy(cslc_ref.at[page, pl.ds(g * HEAD_DIM, HEAD_DIM), :],
                                      kbuf_ref.at[g, :, dst], sem),
                pltpu.make_async_copy(cslc_ref.at[page, pl.ds(gw + g * HEAD_DIM, HEAD_DIM), :],
                                      vbuf_ref.at[g, :, dst], sem))

    for g in range(NSA_KV):
        for r in range(N_SEL_OLD):
            for cp in page_copies(g, r):
                cp.start()
    for g in range(NSA_KV):
        for r in range(N_SEL_OLD):
            for cp in page_copies(g, r):
                cp.wait()

    q = q_ref[...]
    slope = slope_ref[:, 0:1]
    qpos = n_pages * PAGE_SIZE
    hgrp = _iota((NSA_HEADS, 1), 0) // NSA_GROUP
    n_keys = N_SEL_OLD * PAGE_SIZE
    lane = _iota((1, n_keys), 1)
    slot = lane // PAGE_SIZE
    tok = lane % PAGE_SIZE

    def per_group_rows(row):
        k = jnp.where(hgrp == 0, row[:, 0:HEAD_DIM], row[:, HEAD_DIM:2 * HEAD_DIM])
        v = jnp.where(hgrp == 0, row[:, gw:gw + HEAD_DIM], row[:, gw + HEAD_DIM:gw + 2 * HEAD_DIM])
        return k, v

    def finish(s, s_new, pv, v_new):
        mx = jnp.maximum(jnp.max(s, axis=1, keepdims=True), s_new)
        e = jnp.exp(s - mx)
        e_new = jnp.exp(s_new - mx)
        return (pv(e) + e_new * v_new) / (jnp.sum(e, axis=1, keepdims=True) + e_new)

    s_slc = jnp.zeros((NSA_HEADS, n_keys), F32)
    kpos = jnp.zeros((NSA_HEADS, n_keys), jnp.int32)
    valid = jnp.zeros((NSA_HEADS, n_keys), jnp.int32)
    for g in range(NSA_KV):
        s_slc = s_slc + _dot(jnp.where(hgrp == g, q, 0.0), kbuf_ref[g], HI)
        page_pos = jnp.zeros((1, n_keys), jnp.int32)
        half = jnp.zeros((1, n_keys), jnp.int32)
        for r in range(N_SEL_OLD):
            j = idx_ref[b, g * N_SEL_OLD + r]
            page_pos = jnp.where(slot == r, (j // 2) * PAGE_SIZE, page_pos)
            half = jnp.where(slot == r, j % 2, half)
        kpos = jnp.where(hgrp == g, page_pos + tok, kpos)
        valid = jnp.where(hgrp == g, jnp.where((tok // CMP_BLOCK) == half, 1, 0), valid)
    s_slc = jnp.where(valid > 0, s_slc * SCALE - slope * (qpos - kpos).astype(F32), NEG)
    k_new, v_new = per_group_rows(new_slc_ref[...])
    s_new = jnp.sum(q * k_new, axis=1, keepdims=True) * SCALE

    def pv_slc(e):
        return sum(_nt(jnp.where(hgrp == g, e, 0.0), vbuf_ref[g], HI) for g in range(NSA_KV))

    o_slc = finish(s_slc, s_new, pv_slc, v_new)

    cw = cwin_ref[...]
    wlen = cw.shape[1]
    wl = _iota((1, wlen), 1)
    q_groups = jnp.concatenate([jnp.where(hgrp == g, q, 0.0) for g in range(NSA_KV)], axis=1)
    s_w = _dot(q_groups, cw[0:gw], HI) * SCALE - slope * (wlen - wl).astype(F32)
    s_w = jnp.where(wl >= 1, s_w, NEG)
    k_new, v_new = per_group_rows(new_win_ref[...])
    s_new = jnp.sum(q * k_new, axis=1, keepdims=True) * SCALE
    o_win = finish(s_w, s_new, lambda e: _fold_heads(_nt(e, cw[gw:2 * gw], HI), NSA_KV, NSA_GROUP), v_new)

    new_t = new_win_t_ref[...]
    new_col = jnp.sum(jnp.where(_iota((1, new_t.shape[1]), 1) == b, new_t, 0.0), axis=1, keepdims=True)
    swin_ref[...] = jnp.where(wl == wlen - 1, new_col, pltpu.roll(cw, wlen - 1, 1))

    gate = 1.0 / (1.0 + jnp.exp(-gate_ref[...]))
    o_ref[...] = (gate[:, 0:1] * _fold_heads(ocmp_ref[...], NSA_KV, NSA_GROUP) + gate[:, 1:2] * o_slc
                  + gate[:, 2:3] * o_win)


def nsa_sample_attend(page_table, idx, q, gate, o_cmp, new_slc, new_win, new_win_t, c_win_t, c_slc_t):
    bsz, n_pages = page_table.shape
    width, wlen = c_win_t.shape[1:]
    per_b = lambda *shape: pl.BlockSpec((None,) + shape, lambda b, pt, ix: (b, 0, 0))
    n_keys = N_SEL_OLD * PAGE_SIZE
    return pl.pallas_call(
        functools.partial(_nsa_attend_kernel, n_pages=n_pages),
        grid_spec=pltpu.PrefetchScalarGridSpec(
            num_scalar_prefetch=2,
            grid=(bsz,),
            in_specs=[per_b(NSA_HEADS, HEAD_DIM),
                      pl.BlockSpec((NSA_HEADS, 128), lambda b, pt, ix: (0, 0)),
                      per_b(NSA_HEADS, 128), per_b(NSA_HEADS, 128), per_b(1, width), per_b(1, width),
                      pl.BlockSpec((None, width, bsz), lambda b, pt, ix: (0, 0, 0)),
                      per_b(width, wlen),
                      pl.BlockSpec(memory_space=pl.ANY)],
            out_specs=[per_b(NSA_HEADS, HEAD_DIM), per_b(width, wlen)],
            scratch_shapes=[pltpu.VMEM((NSA_KV, HEAD_DIM, n_keys), F32), pltpu.VMEM((NSA_KV, HEAD_DIM, n_keys), F32),
                            pltpu.SemaphoreType.DMA(())]),
        out_shape=[jax.ShapeDtypeStruct((bsz, NSA_HEADS, HEAD_DIM), F32),
                   jax.ShapeDtypeStruct((bsz, width, wlen), F32)],
        compiler_params=_cparams(("arbitrary",)), name="nsa_sample_attend",
    )(page_table, idx, q, _head_slopes(NSA_HEADS), gate, o_cmp, new_slc, new_win, new_win_t, c_win_t, c_slc_t)


def _sb_sample_kernel(pt_ref, q_ref, cache_ref, o_ref, buf_ref, sem, *, n_pages):
    b = pl.program_id(0)
    width = SB_HEADS * HEAD_DIM
    tk = PAGE_SIZE
    shape3 = (SB_HEADS, HEAD_DIM, tk)
    q3 = q_ref[...].reshape(shape3)
    lane = _iota((1, tk), 1)
    slot_of = lambda p: (n_pages - 1 - p) % 2
    page_copy = lambda p: pltpu.make_async_copy(cache_ref.at[pt_ref[b, p]], buf_ref.at[slot_of(p)],
                                                sem.at[slot_of(p)])
    page_copy(n_pages - 1).start()

    def more(state):
        p, _, run = state
        return (p >= 0) & (jnp.max(run) > SB_CUTOFF)

    def body(state):
        p, acc, run = state
        page_copy(p).wait()

        @pl.when(p >= 1)
        def _():
            page_copy(p - 1).start()

        page_ref = buf_ref.at[slot_of(p)]
        z = jnp.sum(page_ref[0:width, :].reshape(shape3) * q3, axis=1)
        sp = _softplus(z)
        ls = -sp
        suffix = ls
        for sh in (1, 2, 4, 8, 16, 32, 64):
            suffix = suffix + jnp.where(lane < tk - sh, pltpu.roll(suffix, tk - sh, 1), 0.0)
        w = jnp.exp(z - sp + (suffix - ls) + run)
        acc = acc + page_ref[width:2 * width, :].reshape(shape3) * w[:, None, :]
        return p - 1, acc, run + suffix[:, 0:1]

    p, acc, _ = lax.while_loop(more, body, (jnp.int32(n_pages - 1), jnp.zeros(shape3, F32),
                                            jnp.zeros((SB_HEADS, 1), F32)))

    @pl.when(p >= 0)
    def _():
        page_copy(p).wait()

    o_ref[...] = jnp.sum(acc.reshape(width, tk), axis=1, keepdims=True)


def sb_sample(page_table, q_lanes, c_sb_t):
    bsz, n_pages = page_table.shape
    width2 = c_sb_t.shape[1]
    return pl.pallas_call(
        functools.partial(_sb_sample_kernel, n_pages=n_pages),
        grid_spec=pltpu.PrefetchScalarGridSpec(
            num_scalar_prefetch=1,
            grid=(bsz,),
            in_specs=[pl.BlockSpec((None, width2 // 2, PAGE_SIZE), lambda b, pt: (b, 0, 0)),
                      pl.BlockSpec(memory_space=pl.ANY)],
            out_specs=pl.BlockSpec((None, width2 // 2, 1), lambda b, pt: (b, 0, 0)),
            scratch_shapes=[pltpu.VMEM((2, width2, PAGE_SIZE), F32), pltpu.SemaphoreType.DMA((2,))]),
        out_shape=jax.ShapeDtypeStruct((bsz, width2 // 2, 1), F32),
        compiler_params=_cparams(("arbitrary",)), name="sb_sample",
    )(page_table, q_lanes, c_sb_t)


def _moba_select_kernel(pt_ref, qm_ref, *refs, n_pages, step):
    page_refs = refs[:step]
    idx_ref, means_ref = refs[step:]
    s = pl.program_id(1)
    pages_per_block = MOBA_BLOCK // PAGE_SIZE
    n_blk = n_pages // pages_per_block
    blane = _iota((1, 128), 1)

    @pl.when(s == 0)
    def _():
        means_ref[...] = jnp.zeros_like(means_ref)

    means = means_ref[...]
    keys = []
    for u, page_ref in enumerate(page_refs):
        keys.append(page_ref[...])
        if len(keys) == pages_per_block:
            mean = jnp.sum(sum(keys[1:], keys[0]), axis=1, keepdims=True) * (1.0 / MOBA_BLOCK)
            means = jnp.where(blane == (s * step + u) // pages_per_block, mean, means)
            keys = []
    means_ref[...] = means

    @pl.when(s == pl.num_programs(1) - 1)
    def _():
        gsc = jnp.where(blane < n_blk, _dot(qm_ref[...], means, HI), NEG)
        blk = blane.astype(F32)
        vec = jnp.zeros((MOBA_HEADS, 128), F32)
        for r in range(MOBA_TOPK):
            best = jnp.max(gsc, axis=1, keepdims=True)
            first = jnp.min(jnp.where(gsc == best, blk, float(BIG_IDX)), axis=1, keepdims=True)
            vec = jnp.where(blane == r, first, vec)
            gsc = jnp.where(blk == first, NEG, gsc)
        idx_ref[...] = vec.astype(jnp.int32)


def moba_sample_select(page_table, qm, c_moba_t):
    bsz, n_pages = page_table.shape
    width = c_moba_t.shape[1] // 2
    step = min(PAGES_PER_STEP, n_pages)
    assert n_pages % step == 0 and MOBA_TOPK <= n_pages * PAGE_SIZE // MOBA_BLOCK <= 128

    def page_spec(u):
        return pl.BlockSpec((None, width, PAGE_SIZE), lambda b, s, pt: (pt[b, s * step + u], 0, 0))

    return pl.pallas_call(
        functools.partial(_moba_select_kernel, n_pages=n_pages, step=step),
        grid_spec=pltpu.PrefetchScalarGridSpec(
            num_scalar_prefetch=1,
            grid=(bsz, n_pages // step),
            in_specs=[pl.BlockSpec((None, MOBA_HEADS, width), lambda b, s, pt: (b, 0, 0))]
                     + [page_spec(u) for u in range(step)],
            out_specs=pl.BlockSpec((None, MOBA_HEADS, 128), lambda b, s, pt: (b, 0, 0)),
            scratch_shapes=[pltpu.VMEM((width, 128), F32)]),
        out_shape=jax.ShapeDtypeStruct((bsz, MOBA_HEADS, 128), jnp.int32),
        compiler_params=_cparams(("parallel", "arbitrary")), name="moba_sample_select",
    )(page_table, qm, *([c_moba_t] * step))


def _moba_attend_kernel(pt_ref, idx_ref, q_ref, knew_ref, vnew_ref, slope_ref, cache_ref, o_ref, kbuf_ref, vbuf_ref,
                        sem, *, n_pages):
    b = pl.program_id(0)
    pages_per_block = MOBA_BLOCK // PAGE_SIZE
    n_sel = MOBA_TOPK * pages_per_block
    width = MOBA_KV * HEAD_DIM

    def page_copies(h, u):
        g = h // MOBA_GROUP
        page = pt_ref[b, idx_ref[b, h * MOBA_TOPK + u // pages_per_block] * pages_per_block + u % pages_per_block]
        dst = pl.ds(u * PAGE_SIZE, PAGE_SIZE)
        return (pltpu.make_async_copy(cache_ref.at[page, pl.ds(g * HEAD_DIM, HEAD_DIM), :],
                                      kbuf_ref.at[h, :, dst], sem),
                pltpu.make_async_copy(cache_ref.at[page, pl.ds(width + g * HEAD_DIM, HEAD_DIM), :],
                                      vbuf_ref.at[h, :, dst], sem))

    for h in range(MOBA_HEADS):
        for u in range(n_sel):
            for cp in page_copies(h, u):
                cp.start()
    for h in range(MOBA_HEADS):
        for u in range(n_sel):
            for cp in page_copies(h, u):
                cp.wait()

    n_keys = n_sel * PAGE_SIZE
    shape3 = (MOBA_HEADS, HEAD_DIM, PAGE_SIZE)
    q3 = q_ref[...].reshape(shape3)
    z = jnp.sum(kbuf_ref[...] * jnp.concatenate([q3] * n_sel, axis=2), axis=1)
    lane = _iota((1, n_keys), 1)
    hrow = _iota((MOBA_HEADS, 1), 0)
    kpos = jnp.zeros((MOBA_HEADS, n_keys), jnp.int32)
    for h in range(MOBA_HEADS):
        row = lane % MOBA_BLOCK
        for r in range(MOBA_TOPK):
            row = row + jnp.where(lane // MOBA_BLOCK == r, idx_ref[b, h * MOBA_TOPK + r] * MOBA_BLOCK, 0)
        kpos = jnp.where(hrow == h, row, kpos)
    qpos = n_pages * PAGE_SIZE
    sc = z - slope_ref[:, 0:1] * (qpos - kpos).astype(F32)
    knew = knew_ref[...].reshape(MOBA_HEADS, HEAD_DIM, 1)
    s_new = jnp.sum(q3[:, :, 0:1] * knew, axis=1)
    mx = jnp.maximum(jnp.max(sc, axis=1, keepdims=True), s_new)
    e = jnp.exp(sc - mx)
    e_new = jnp.exp(s_new - mx)
    den = jnp.sum(e, axis=1, keepdims=True) + e_new
    pv = vbuf_ref[...] * e[:, None, :]
    tiles = pv[:, :, 0:PAGE_SIZE]
    for u in range(1, n_sel):
        tiles = tiles + pv[:, :, u * PAGE_SIZE:(u + 1) * PAGE_SIZE]
    o = jnp.sum(tiles, axis=2, keepdims=True) + e_new[:, :, None] * vnew_ref[...].reshape(MOBA_HEADS, HEAD_DIM, 1)
    o_ref[...] = (o / den[:, :, None]).reshape(MOBA_HEADS * HEAD_DIM, 1)


def moba_sample_attend(page_table, idx, q_lanes, knew, vnew, c_moba_t):
    bsz, n_pages = page_table.shape
    hd = MOBA_HEADS * HEAD_DIM
    n_keys = MOBA_TOPK * MOBA_BLOCK
    per_b = lambda *shape: pl.BlockSpec((None,) + shape, lambda b, pt, ix: (b, 0, 0))
    return pl.pallas_call(
        functools.partial(_moba_attend_kernel, n_pages=n_pages),
        grid_spec=pltpu.PrefetchScalarGridSpec(
            num_scalar_prefetch=2,
            grid=(bsz,),
            in_specs=[per_b(hd, PAGE_SIZE), per_b(hd, 1), per_b(hd, 1),
                      pl.BlockSpec((MOBA_HEADS, 128), lambda b, pt, ix: (0, 0)),
                      pl.BlockSpec(memory_space=pl.ANY)],
            out_specs=per_b(hd, 1),
            scratch_shapes=[pltpu.VMEM((MOBA_HEADS, HEAD_DIM, n_keys), F32),
                            pltpu.VMEM((MOBA_HEADS, HEAD_DIM, n_keys), F32),
                            pltpu.SemaphoreType.DMA(())]),
        out_shape=jax.ShapeDtypeStruct((bsz, hd, 1), F32),
        compiler_params=_cparams(("arbitrary",)), name="moba_sample_attend",
    )(page_table, idx, q_lanes, knew, vnew, _head_slopes(MOBA_HEADS), c_moba_t)


def _block_diag_query(q, n_groups):
    bsz, n_heads, d = q.shape
    onehot = (jnp.arange(n_heads)[:, None] // (n_heads // n_groups) == jnp.arange(n_groups)[None, :]).astype(q.dtype)
    return (q[:, :, None, :] * onehot[None, :, :, None]).reshape(bsz, n_heads, n_groups * d)


def _pages_feature_major(cache):
    n_phys, page = cache.shape[:2]
    return jnp.transpose(cache, (0, 2, 3, 4, 1)).reshape(n_phys, -1, page)


def _ab_sample_weights(w):
    g0, g1 = AB_COLS["gate"]
    w_std = jnp.concatenate([w[:, :g0], jnp.pad(w[:, g0:g1], ((0, 0), (0, 128 - N_GATE))), w[:, g1:]],
                            axis=1).astype(BF16)
    w_t = w.T[AB_COLS["win"][0]:AB_COLS["win"][1]].astype(BF16)
    return w_std, w_t


def ab_sample_mix(y, g_mix, w_std, w_t, c_slc, c_cmp, c_win, c_sb, page_table):
    bsz = y.shape[0]
    n_phys = c_slc.shape[0]
    std = []
    off = 0
    for wd in AB_WIDTHS:
        std.append((off, wd, F32))
        off += wd
    width = 2 * NSA_KV * HEAD_DIM
    q_a, kv_cmp, kv_slc, kv_win, gate, q_b, kv_b, kvw_t = rms_proj(y, g_mix, w_std, std, w_t, ((0, width, F32),),
                                                                   1, bsz, bsz)
    q_a = q_a.reshape(bsz, NSA_HEADS, HEAD_DIM)
    qm_kv = jnp.pad(_block_diag_query(q_a, NSA_KV), ((0, 0), (0, 0), (0, NSA_KV * HEAD_DIM)))
    zeros = jnp.zeros_like(qm_kv)
    qm_cmp = jnp.concatenate([jnp.concatenate([qm_kv, zeros], -1), jnp.concatenate([zeros, qm_kv], -1)], axis=1)
    o_cmp, idx = nsa_sample_select(page_table, qm_cmp, c_cmp.reshape(n_phys, -1))
    gate3 = jnp.pad(gate[:, :N_GATE].reshape(bsz, NSA_HEADS, 3), ((0, 0), (0, 0), (0, 125)))
    c_win_t = jnp.transpose(c_win, (0, 2, 3, 4, 1)).reshape(bsz, width, -1)
    o_a, s_win_t = nsa_sample_attend(page_table, idx[:, :NSA_KV, :N_SEL_OLD].reshape(bsz, -1), q_a, gate3, o_cmp,
                                     kv_slc.reshape(bsz, 1, width), kv_win.reshape(bsz, 1, width), kvw_t,
                                     c_win_t, _pages_feature_major(c_slc))
    q_lanes = jnp.broadcast_to((q_b * SCALE)[:, :, None], (bsz, SB_HEADS * HEAD_DIM, PAGE_SIZE))
    o_b = sb_sample(page_table, q_lanes, _pages_feature_major(c_sb))
    o_ts = jnp.concatenate([o_a.reshape(bsz, -1), o_b.reshape(bsz, -1)], axis=1).T[None].astype(BF16)
    shape5 = (bsz, 1, 2, NSA_KV, HEAD_DIM)
    s_win = jnp.transpose(s_win_t.reshape(bsz, 2, NSA_KV, HEAD_DIM, -1), (0, 4, 1, 2, 3))
    return (o_ts, kv_slc.reshape(shape5), kv_cmp.reshape(shape5), s_win,
            kv_b.reshape(bsz, 1, 2, SB_HEADS, HEAD_DIM))


def c_sample_mix(y, g_mix, w_std, c_moba, page_table):
    bsz = y.shape[0]
    q, kv = rms_proj(y, g_mix, w_std, ((0, 1024, F32), (1024, 512, F32)), None, (), 1, bsz, bsz)
    qm = _block_diag_query(q.reshape(bsz, MOBA_HEADS, HEAD_DIM), MOBA_KV)
    pages = _pages_feature_major(c_moba)
    idx = moba_sample_select(page_table, qm, pages)[:, :, :MOBA_TOPK].reshape(bsz, -1)
    hd = MOBA_HEADS * HEAD_DIM
    q_lanes = jnp.broadcast_to((q * SCALE)[:, :, None], (bsz, hd, PAGE_SIZE))
    per_head = lambda x: jnp.broadcast_to(x.reshape(bsz, MOBA_KV, 1, HEAD_DIM),
                                          (bsz, MOBA_KV, MOBA_GROUP, HEAD_DIM)).reshape(bsz, hd, 1)
    kw = MOBA_KV * HEAD_DIM
    o = moba_sample_attend(page_table, idx, q_lanes, per_head(kv[:, :kw]), per_head(kv[:, kw:]), pages)
    return o.reshape(bsz, hd).T[None].astype(BF16), kv.reshape(bsz, 1, 2, MOBA_KV, HEAD_DIM)


def kernel(x_prompt, x_sample, cache_nsa_slc, cache_nsa_cmp, cache_nsa_win, cache_sb, cache_moba, page_table,
           norm_mix, w_in_ab, w_out_ab, cmp_pos, cmp_w1, cmp_w2, w_in_c, w_out_c,
           norm_ffn, w_up, w_down, norm_final):
    bsz, seq, d = x_prompt.shape
    tm = 512
    w_up_b = w_up.astype(BF16)
    w_down_b = w_down.astype(BF16)
    dbsz = x_sample.shape[0]
    yp = x_prompt.reshape(bsz * seq, d)
    ys = x_sample.reshape(dbsz, d)
    w_oab = w_out_ab[0].astype(BF16)
    w_oc = w_out_c[0].astype(BF16)
    wc = w_in_c[0]
    k_lo, k_hi = MOBA_HEADS * HEAD_DIM, (MOBA_HEADS + MOBA_KV) * HEAD_DIM

    ab_std, ab_t = _ab_prompt_weights(w_in_ab[0])
    o_ts, p_slc, p_cmp, p_win, p_sb = ab_prompt_mix(yp, norm_mix[0], ab_std, ab_t, cmp_pos[0], cmp_w1[0], cmp_w2[0],
                                                    bsz, seq, tm)
    yp = mix_mlp(yp, o_ts, w_oab, norm_ffn[0], w_up_b[0], w_down_b[0], norm_final, False, bsz, seq, tm)
    o_ts, p_moba = c_prompt_mix(yp, norm_mix[1], wc[:, k_lo:k_hi].astype(BF16), wc.T.astype(BF16), bsz, seq, tm)
    yp = mix_mlp(yp, o_ts, w_oc, norm_ffn[1], w_up_b[1], w_down_b[1], norm_final, True, bsz, seq, tm)

    abs_std, abs_t = _ab_sample_weights(w_in_ab[0])
    o_ts, s_slc, s_cmp, s_win, s_sb = ab_sample_mix(ys, norm_mix[0], abs_std, abs_t, cache_nsa_slc[0],
                                                    cache_nsa_cmp[0], cache_nsa_win[0], cache_sb[0], page_table)
    ys = mix_mlp(ys, o_ts, w_oab, norm_ffn[0], w_up_b[0], w_down_b[0], norm_final, False, 1, dbsz, dbsz)
    o_ts, s_moba = c_sample_mix(ys, norm_mix[1], wc.astype(BF16), cache_moba[0], page_table)
    ys = mix_mlp(ys, o_ts, w_oc, norm_ffn[1], w_up_b[1], w_down_b[1], norm_final, True, 1, dbsz, dbsz)

    return (yp.reshape(bsz, seq, d), ys.reshape(dbsz, 1, d), p_slc[None], p_cmp[None], p_win[None], p_sb[None],
            p_moba[None], s_slc[None], s_cmp[None], s_win[None], s_sb[None], s_moba[None])
```
